```python
import jax
import jax.numpy as jnp
from jax import lax
import numpy as np

D_MODEL = 2048
BATCH = 2
SEQ = 4096
DEPTH = 2

GRID_W = 64
CTX_LEN = 256
NORM_EPS = 1e-6

RWKV_HEAD_DIM = 64
RWKV_WIDTH = D_MODEL // 2
RWKV_HEADS = RWKV_WIDTH // RWKV_HEAD_DIM
DECAY_LORA = 64
ICLR_LORA = 64
GATE_LORA = 128
RWKV_GN_EPS = 64e-5
RWKV_COLS = 3 * RWKV_WIDTH + 2 * DECAY_LORA + 2 * ICLR_LORA + GATE_LORA

NA_HEAD_DIM = 64
NA_WIDTH = D_MODEL - RWKV_WIDTH
NA_HEADS = NA_WIDTH // NA_HEAD_DIM
NA_KH_MAX = 8
NA_KW = 16
L0_IN_COLS = RWKV_COLS + 3 * NA_WIDTH

GLA_HEADS = 4
GLA_KEY_WIDTH = D_MODEL // 2
GLA_VAL_WIDTH = D_MODEL
GLA_DK = GLA_KEY_WIDTH // GLA_HEADS
GLA_DV = GLA_VAL_WIDTH // GLA_HEADS
GLA_GATE_RANK = 16
GLA_GATE_NORMALIZER = 16.0
GLA_CHUNK = 64
GLA_STATE_COLS = GLA_KEY_WIDTH + GLA_VAL_WIDTH + 2 * GLA_GATE_RANK
L1_IN_COLS = GLA_STATE_COLS + GLA_KEY_WIDTH + GLA_VAL_WIDTH

FFN_DIM = 5632
N_EXPERTS = 8
TOP_K = 2
EXPERT_DIM = 7168

kernel_name = 'hybrid_rwkv7_natten_gla_moe_dit_block'


def rms_norm(x, gain):
    xf = x.astype(jnp.float32)
    y = xf * lax.rsqrt(jnp.mean(xf * xf, axis=-1, keepdims=True) + NORM_EPS)
    return (y * gain.astype(jnp.float32)).astype(x.dtype)


def adaln(cond, w_mod, b_mod):
    return jnp.split(jax.nn.silu(cond) @ w_mod + b_mod, 6, axis=-1)


def modulate(h, shift, scale):
    return h * (1 + scale) + shift


def swiglu(h, w_gate, w_up, w_down):
    return (jax.nn.silu(h @ w_gate) * (h @ w_up)) @ w_down


def bidir_token_shift(p, mu):
    prev = jnp.pad(p[:, :-1], ((0, 0), (1, 0), (0, 0)))
    nxt = jnp.pad(p[:, 1:], ((0, 0), (0, 1), (0, 0)))
    return p + mu * (0.5 * (prev + nxt) - p)


def rwkv7_prepare(pr, decay_w0, decay_w2, iclr_a0, iclr_a2, gate_g2, k_k, k_a):
    B, T, _ = pr.shape
    W = RWKV_WIDTH

    def heads(t):
        return t.reshape(B, T, RWKV_HEADS, RWKV_HEAD_DIM)

    r, k, v = pr[..., :W], pr[..., W:2 * W], pr[..., 2 * W:3 * W]
    o = 3 * W
    dec_in = (pr[..., o:o + DECAY_LORA], pr[..., o + DECAY_LORA:o + 2 * DECAY_LORA])
    o += 2 * DECAY_LORA
    icl_in = (pr[..., o:o + ICLR_LORA], pr[..., o + ICLR_LORA:o + 2 * ICLR_LORA])
    o += 2 * ICLR_LORA
    g = jax.nn.sigmoid(pr[..., o:o + GATE_LORA]) @ gate_g2
    kk = heads(k * k_k).astype(jnp.float32)
    kk = kk * lax.rsqrt(jnp.sum(kk * kk, axis=-1, keepdims=True) + 1e-12)
    dirs = []
    for d in range(2):
        w_log = -jax.nn.softplus(-(decay_w0[d] + jnp.tanh(dec_in[d]) @ decay_w2[d])) - 0.5
        decay = jnp.exp(-jnp.exp(w_log.astype(jnp.float32)))
        a = jax.nn.sigmoid(iclr_a0[d] + icl_in[d] @ iclr_a2[d])
        k_d = k * (1 + (a - 1) * k_a)
        dirs.append((heads(decay), heads(k_d), heads(a)))
    return heads(r), heads(v), kk, g, dirs


def rwkv7_scan(r, decay, k, v, kk, a, s0, reverse):
    def step(S, inp):
        r_t, w_t, k_t, v_t, kk_t, a_t = inp
        sa = jnp.einsum('bhvk,bhk->bhv', S, -kk_t)
        S = (S * w_t[:, :, None, :] + sa[..., None] * (kk_t * a_t)[:, :, None, :]
             + v_t[..., None] * k_t[:, :, None, :])
        return S, jnp.einsum('bhvk,bhk->bhv', S, r_t)

    xs = tuple(jnp.moveaxis(t.astype(jnp.float32), 1, 0) for t in (r, decay, k, v, kk, a))
    s_final, ys = lax.scan(step, s0, xs, reverse=reverse)
    return jnp.moveaxis(ys, 0, 1), s_final


def rwkv7_output(y, r, k_bonus, v, g, r_k, lnx_g, lnx_b):
    B, T, H, N = y.shape
    mu = jnp.mean(y, axis=-1, keepdims=True)
    var = jnp.mean(jnp.square(y - mu), axis=-1, keepdims=True)
    yn = ((y - mu) * lax.rsqrt(var + RWKV_GN_EPS)).astype(r.dtype).reshape(B, T, H * N)
    yn = yn * lnx_g + lnx_b
    bonus = (jnp.sum(r * k_bonus * r_k, axis=-1, keepdims=True) * v).reshape(B, T, H * N)
    return (yn + bonus) * g


def rwkv7_group(pr_l, pr_c, decay_w0, decay_w2, iclr_a0, iclr_a2, gate_g2, k_k, k_a, r_k, lnx_g, lnx_b):
    lora = (decay_w0, decay_w2, iclr_a0, iclr_a2, gate_g2, k_k, k_a)
    r_l, v_l, kk_l, g_l, dirs_l = rwkv7_prepare(pr_l, *lora)
    r_c, v_c, kk_c, g_c, dirs_c = rwkv7_prepare(pr_c, *lora)
    B = pr_l.shape[0]
    y_l = 0.0
    y_c = 0.0
    for d, reverse in enumerate((False, True)):
        s0 = jnp.zeros((B, RWKV_HEADS, RWKV_HEAD_DIM, RWKV_HEAD_DIM), jnp.float32)
        dec_c, k_dc, a_c = dirs_c[d]
        yc, s_ctx = rwkv7_scan(r_c, dec_c, k_dc, v_c, kk_c, a_c, s0, reverse)
        dec_l, k_dl, a_l = dirs_l[d]
        yl, _ = rwkv7_scan(r_l, dec_l, k_dl, v_l, kk_l, a_l, s_ctx, reverse)
        y_c = y_c + yc
        y_l = y_l + yl
    out_l = rwkv7_output(y_l, r_l, 0.5 * (dirs_l[0][1] + dirs_l[1][1]), v_l, g_l, r_k, lnx_g, lnx_b)
    out_c = rwkv7_output(y_c, r_c, 0.5 * (dirs_c[0][1] + dirs_c[1][1]), v_c, g_c, r_k, lnx_g, lnx_b)
    return out_l, out_c


def na_qkv(p, q_norm, k_norm):
    B, T, _ = p.shape
    q, k, v = jnp.split(p, 3, axis=-1)
    sh = (B, T, NA_HEADS, NA_HEAD_DIM)
    return rms_norm(q.reshape(sh), q_norm), rms_norm(k.reshape(sh), k_norm), v.reshape(sh)


def neighbourhood_attention(q, k, v, k_ctx, v_ctx, rpb):
    B, T, H, N = q.shape
    rows = T // GRID_W
    kh = min(NA_KH_MAX, rows)
    kw = min(NA_KW, GRID_W)
    scale = N ** -0.5
    qg = q.reshape(B, rows, GRID_W, H, N)
    kg = k.reshape(B, rows, GRID_W, H, N)
    vg = v.reshape(B, rows, GRID_W, H, N)
    row_start = jnp.clip(jnp.arange(rows) - kh // 2, 0, rows - kh)
    col_pos = jnp.arange(GRID_W)
    col_idx = jnp.clip(col_pos - kw // 2, 0, GRID_W - kw)[:, None] + jnp.arange(kw)[None, :]
    col_bias_idx = col_idx - col_pos[:, None] + NA_KW - 1

    def one_row(args):
        q_row, r, r0 = args
        k_band = lax.dynamic_slice_in_dim(kg, r0, kh, axis=1)
        v_band = lax.dynamic_slice_in_dim(vg, r0, kh, axis=1)
        k_win = k_band[:, :, col_idx]
        v_win = v_band[:, :, col_idx]
        row_bias_idx = r0 + jnp.arange(kh) - r + NA_KH_MAX - 1
        bias = rpb[:, row_bias_idx[:, None, None], col_bias_idx[None, :, :]]
        s_win = (jnp.einsum('bqhn,biqjhn->bhqij', q_row, k_win).astype(jnp.float32) * scale
                 + bias.transpose(0, 2, 1, 3).astype(jnp.float32)[None])
        s_ctx = jnp.einsum('bqhn,bchn->bhqc', q_row, k_ctx).astype(jnp.float32) * scale
        s = jnp.concatenate([s_win.reshape(B, H, GRID_W, kh * kw), s_ctx], axis=-1)
        p = jax.nn.softmax(s, axis=-1).astype(v.dtype)
        p_win = p[..., :kh * kw].reshape(B, H, GRID_W, kh, kw)
        p_ctx = p[..., kh * kw:]
        return (jnp.einsum('bhqij,biqjhn->bqhn', p_win, v_win)
                + jnp.einsum('bhqc,bchn->bqhn', p_ctx, v_ctx))

    out = lax.map(one_row, (jnp.moveaxis(qg, 1, 0), jnp.arange(rows), row_start))
    return jnp.moveaxis(out, 0, 1).reshape(B, T, H * N)


def context_attention(q, k, v):
    B, C, H, N = q.shape
    s = jnp.einsum('bqhn,bkhn->bhqk', q, k).astype(jnp.float32) * (N ** -0.5)
    p = jax.nn.softmax(s, axis=-1).astype(v.dtype)
    return jnp.einsum('bhqk,bkhn->bqhn', p, v).reshape(B, C, H * N)


def gla_forget_gate(lr, w2, b):
    return jax.nn.log_sigmoid((lr @ w2 + b).astype(jnp.float32)) / GLA_GATE_NORMALIZER


def gla_final_state(k, v, g):
    b = jnp.cumsum(g.astype(jnp.float32), axis=1)
    return jnp.einsum('bthd,bthv->bhdv', k.astype(jnp.float32) * jnp.exp(b[:, -1:] - b), v.astype(jnp.float32))


def chunked_gla(q, k, v, g, s0):
    B, T, H, DK = q.shape
    DV = v.shape[-1]
    n = T // GLA_CHUNK

    def chunks(t):
        return t.astype(jnp.float32).reshape(B, n, GLA_CHUNK, H, t.shape[-1]).transpose(1, 0, 3, 2, 4)

    lower = jnp.tril(jnp.ones((GLA_CHUNK, GLA_CHUNK), dtype=bool))

    def step(S, inp):
        qc, kc, vc, gc = inp
        b = jnp.cumsum(gc, axis=2)
        o_inter = jnp.einsum('bhtd,bhdv->bhtv', qc * jnp.exp(b), S)
        rel = jnp.where(lower[:, :, None], b[:, :, :, None, :] - b[:, :, None, :, :], -jnp.inf)
        att = jnp.einsum('bhtd,bhsd,bhtsd->bhts', qc, kc, jnp.exp(rel))
        o = o_inter + jnp.einsum('bhts,bhsv->bhtv', att, vc)
        b_end = b[:, :, -1:, :]
        S = (jnp.exp(b_end)[:, :, 0, :, None] * S
             + jnp.einsum('bhsd,bhsv->bhdv', kc * jnp.exp(b_end - b), vc))
        return S, o

    s_final, o = lax.scan(step, s0, (chunks(q), chunks(k), chunks(v), chunks(g)))
    return o.transpose(1, 0, 3, 2, 4).reshape(B, T, H, DV), s_final


def gla_group(h_l, h_c, w_in, gate_w2, gate_b, o_norm):
    B, T, _ = h_l.shape
    C = h_c.shape[1]
    p_l = h_l @ w_in
    p_c = h_c @ w_in[:, :GLA_STATE_COLS]

    def split_state(p):
        Bp, Tp, _ = p.shape
        k = p[..., :GLA_KEY_WIDTH].reshape(Bp, Tp, GLA_HEADS, GLA_DK)
        v = p[..., GLA_KEY_WIDTH:GLA_KEY_WIDTH + GLA_VAL_WIDTH].reshape(Bp, Tp, GLA_HEADS, GLA_DV)
        o = GLA_KEY_WIDTH + GLA_VAL_WIDTH
        lr = (p[..., o:o + GLA_GATE_RANK], p[..., o + GLA_GATE_RANK:o + 2 * GLA_GATE_RANK])
        return k, v, lr

    k_l, v_l, lr_l = split_state(p_l)
    k_c, v_c, lr_c = split_state(p_c)
    q_l = p_l[..., GLA_STATE_COLS:GLA_STATE_COLS + GLA_KEY_WIDTH].reshape(B, T, GLA_HEADS, GLA_DK) * (GLA_DK ** -0.5)
    out_gate = p_l[..., GLA_STATE_COLS + GLA_KEY_WIDTH:]
    o = 0.0
    for d in range(2):
        if d == 0:
            tdir = lambda t: t
        else:
            tdir = lambda t: jnp.flip(t, axis=1)
        g_l = gla_forget_gate(lr_l[d], gate_w2[d], gate_b[d]).reshape(B, T, GLA_HEADS, GLA_DK)
        g_c = gla_forget_gate(lr_c[d], gate_w2[d], gate_b[d]).reshape(B, C, GLA_HEADS, GLA_DK)
        s_ctx = gla_final_state(tdir(k_c), tdir(v_c), tdir(g_c))
        o_d, _ = chunked_gla(tdir(q_l), tdir(k_l), tdir(v_l), tdir(g_l), s_ctx)
        o = o + tdir(o_d)
    o = rms_norm(o, o_norm).reshape(B, T, GLA_VAL_WIDTH)
    return (o * jax.nn.silu(out_gate.astype(jnp.float32))).astype(h_l.dtype)


def moe_swiglu(h, router, exp_wg, exp_wu, exp_wd):
    B, T, D = h.shape
    hf = h.reshape(B * T, D)
    logits = (hf @ router).astype(jnp.float32)
    top_val, top_idx = lax.top_k(logits, TOP_K)
    top_w = jax.nn.softmax(top_val, axis=-1)
    combine = jnp.sum(jax.nn.one_hot(top_idx, N_EXPERTS, dtype=jnp.float32) * top_w[..., None], axis=1)
    y = jnp.zeros_like(hf)
    for e in range(N_EXPERTS):
        y = y + combine[:, e:e + 1].astype(h.dtype) * swiglu(hf, exp_wg[e], exp_wu[e], exp_wd[e])
    return y.reshape(B, T, D)


def even_layer(x, ctx, c, c_ctx, w_mod, b_mod, norm1, norm2, w_in, mu_shift,
               decay_w0, decay_w2, iclr_a0, iclr_a2, gate_g2, k_k, k_a, r_k, lnx_g, lnx_b,
               q_norm, k_norm, rpb, w_out, ffn_wg, ffn_wu, ffn_wd):
    sh1, sc1, gt1, sh2, sc2, gt2 = [m[:, None, :] for m in adaln(c, w_mod, b_mod)]
    csh1, csc1, cgt1, csh2, csc2, cgt2 = adaln(c_ctx, w_mod, b_mod)
    h_l = modulate(rms_norm(x, norm1), sh1, sc1)
    h_c = modulate(rms_norm(ctx, norm1), csh1, csc1)
    p_l = h_l @ w_in
    p_c = h_c @ w_in
    a_l, a_c = rwkv7_group(bidir_token_shift(p_l[..., :RWKV_COLS], mu_shift),
                           bidir_token_shift(p_c[..., :RWKV_COLS], mu_shift),
                           decay_w0, decay_w2, iclr_a0, iclr_a2, gate_g2, k_k, k_a, r_k, lnx_g, lnx_b)
    q_l, k_l, v_l = na_qkv(p_l[..., RWKV_COLS:], q_norm, k_norm)
    q_c, k_c, v_c = na_qkv(p_c[..., RWKV_COLS:], q_norm, k_norm)
    b_l = neighbourhood_attention(q_l, k_l, v_l, k_c, v_c, rpb)
    b_c = context_attention(q_c, k_c, v_c)
    x = x + gt1 * (jnp.concatenate([a_l, b_l], axis=-1) @ w_out)
    ctx = ctx + cgt1 * (jnp.concatenate([a_c, b_c], axis=-1) @ w_out)
    x = x + gt2 * swiglu(modulate(rms_norm(x, norm2), sh2, sc2), ffn_wg, ffn_wu, ffn_wd)
    ctx = ctx + cgt2 * swiglu(modulate(rms_norm(ctx, norm2), csh2, csc2), ffn_wg, ffn_wu, ffn_wd)
    return x, ctx


def odd_layer(x, ctx, c, c_ctx, w_mod, b_mod, norm1, norm2, w_in, gate_w2, gate_b, o_norm, w_out,
              router, exp_wg, exp_wu, exp_wd):
    D = x.shape[-1]
    sh1, sc1, gt1, sh2, sc2, gt2 = [m[:, None, :] for m in adaln(c, w_mod, b_mod)]
    csh1, csc1 = jnp.split(jax.nn.silu(c_ctx) @ w_mod[:, :2 * D] + b_mod[:2 * D], 2)
    h_l = modulate(rms_norm(x, norm1), sh1, sc1)
    h_c = modulate(rms_norm(ctx, norm1), csh1, csc1)
    x = x + gt1 * (gla_group(h_l, h_c, w_in, gate_w2, gate_b, o_norm) @ w_out)
    h = modulate(rms_norm(x, norm2), sh2, sc2)
    return x + gt2 * moe_swiglu(h, router, exp_wg, exp_wu, exp_wd)


def setup_inputs(seed: int = 0) -> dict:
    key = jax.random.key(seed)
    ks = iter(jax.random.split(key, 64))
    D = D_MODEL

    def nrm(shape, scale):
        return scale * jax.random.normal(next(ks), shape, jnp.float32)

    def gain(n):
        return 1.0 + nrm((n,), 0.02)

    inputs = {
        'x': nrm((BATCH, SEQ, D), 1.0),
        'c': nrm((BATCH, D), 1.0),
        'ctx': nrm((BATCH, CTX_LEN, D), 1.0),
        'c_ctx': nrm((D,), 1.0),
        'l0_w_mod': nrm((D, 6 * D), 0.5 * D ** -0.5),
        'l0_b_mod': nrm((6 * D,), 0.02),
        'l0_norm1': gain(D),
        'l0_norm2': gain(D),
        'l0_w_in': nrm((D, L0_IN_COLS), D ** -0.5),
        'l0_mu_shift': jax.random.uniform(next(ks), (RWKV_COLS,), jnp.float32, 0.1, 0.9),
        'l0_decay_w0': nrm((2, RWKV_WIDTH), 0.5),
        'l0_decay_w2': nrm((2, DECAY_LORA, RWKV_WIDTH), 0.5 * DECAY_LORA ** -0.5),
        'l0_iclr_a0': nrm((2, RWKV_WIDTH), 0.1),
        'l0_iclr_a2': nrm((2, ICLR_LORA, RWKV_WIDTH), ICLR_LORA ** -0.5),
        'l0_gate_g2': nrm((GATE_LORA, RWKV_WIDTH), GATE_LORA ** -0.5),
        'l0_k_k': 0.85 + nrm((RWKV_WIDTH,), 0.02),
        'l0_k_a': gain(RWKV_WIDTH),
        'l0_r_k': nrm((RWKV_HEADS, RWKV_HEAD_DIM), 0.1),
        'l0_lnx_g': gain(RWKV_WIDTH),
        'l0_lnx_b': nrm((RWKV_WIDTH,), 0.02),
        'l0_q_norm': gain(NA_HEAD_DIM),
        'l0_k_norm': gain(NA_HEAD_DIM),
        'l0_rpb': nrm((NA_HEADS, 2 * NA_KH_MAX - 1, 2 * NA_KW - 1), 0.1),
        'l0_w_out': nrm((RWKV_WIDTH + NA_WIDTH, D), (RWKV_WIDTH + NA_WIDTH) ** -0.5),
        'l0_ffn_wg': nrm((D, FFN_DIM), D ** -0.5),
        'l0_ffn_wu': nrm((D, FFN_DIM), D ** -0.5),
        'l0_ffn_wd': nrm((FFN_DIM, D), FFN_DIM ** -0.5),
        'l1_w_mod': nrm((D, 6 * D), 0.5 * D ** -0.5),
        'l1_b_mod': nrm((6 * D,), 0.02),
        'l1_norm1': gain(D),
        'l1_norm2': gain(D),
        'l1_w_in': nrm((D, L1_IN_COLS), D ** -0.5),
        'l1_gate_w2': nrm((2, GLA_GATE_RANK, GLA_KEY_WIDTH), GLA_GATE_RANK ** -0.5),
        'l1_gate_b': nrm((2, GLA_KEY_WIDTH), 0.1),
        'l1_o_norm': gain(GLA_DV),
        'l1_w_out': nrm((GLA_VAL_WIDTH, D), GLA_VAL_WIDTH ** -0.5),
        'l1_router': nrm((D, N_EXPERTS), D ** -0.5),
        'l1_exp_wg': nrm((N_EXPERTS, D, EXPERT_DIM), D ** -0.5),
        'l1_exp_wu': nrm((N_EXPERTS, D, EXPERT_DIM), D ** -0.5),
        'l1_exp_wd': nrm((N_EXPERTS, EXPERT_DIM, D), EXPERT_DIM ** -0.5),
    }
    return inputs


def reference(x, c, ctx, c_ctx,
              l0_w_mod, l0_b_mod, l0_norm1, l0_norm2, l0_w_in, l0_mu_shift,
              l0_decay_w0, l0_decay_w2, l0_iclr_a0, l0_iclr_a2, l0_gate_g2,
              l0_k_k, l0_k_a, l0_r_k, l0_lnx_g, l0_lnx_b,
              l0_q_norm, l0_k_norm, l0_rpb, l0_w_out,
              l0_ffn_wg, l0_ffn_wu, l0_ffn_wd,
              l1_w_mod, l1_b_mod, l1_norm1, l1_norm2, l1_w_in,
              l1_gate_w2, l1_gate_b, l1_o_norm, l1_w_out,
              l1_router, l1_exp_wg, l1_exp_wu, l1_exp_wd):
    layer_params = (
        (l0_w_mod, l0_b_mod, l0_norm1, l0_norm2, l0_w_in, l0_mu_shift,
         l0_decay_w0, l0_decay_w2, l0_iclr_a0, l0_iclr_a2, l0_gate_g2,
         l0_k_k, l0_k_a, l0_r_k, l0_lnx_g, l0_lnx_b,
         l0_q_norm, l0_k_norm, l0_rpb, l0_w_out, l0_ffn_wg, l0_ffn_wu, l0_ffn_wd),
        (l1_w_mod, l1_b_mod, l1_norm1, l1_norm2, l1_w_in,
         l1_gate_w2, l1_gate_b, l1_o_norm, l1_w_out,
         l1_router, l1_exp_wg, l1_exp_wu, l1_exp_wd),
    )
    for layer in range(DEPTH):
        if layer % 2 == 0:
            x, ctx = even_layer(x, ctx, c, c_ctx, *layer_params[layer])
        else:
            x = odd_layer(x, ctx, c, c_ctx, *layer_params[layer])
    return x
```

```python
import functools

import jax
import jax.numpy as jnp
from jax import lax
from jax.experimental import pallas as pl
from jax.experimental.pallas import tpu as pltpu

F32 = jnp.float32
BF16 = jnp.bfloat16
I32 = jnp.int32
U32 = jnp.uint32

D = 2048
B = 2
T = 4096
C = 256
NL = B * T
NC = B * C
NR = NL + NC
GRID_W = 64
GRID_H = T // GRID_W
NORM_EPS = 1e-6

HD = 64
RW = 1024
RWKV_COLS = 3 * RW + 2 * 64 + 2 * 64 + 128
NA_W = 1024
NA_KH = 8
NA_KW = 16
RWKV_GN_EPS = 64e-5

GLA_H = 4
GLA_DK = 256
GLA_DV = 512
GLA_KW = 1024
GLA_VW = 2048
GLA_RANK = 16
GLA_NORMALIZER = 16.0
GLA_CHUNK = 64
GLA_SUB = 16

FFN = 5632
NE = 8
EDIM = 7168

LANES = 128
VMEM_BYTES = 64 * 1024 * 1024
VMEM_CAP = VMEM_BYTES - 8 * 1024 * 1024

TM = 512
SEG_TILES = T // TM
PREP_TM = 256
SCAN_TB = 64
NPAIR = RW // LANES
GLA_TB = 512
K10_TM = 256
MOE_SUB = 256
MOE_NSUB = 4
MOE_TF = 256
NASSIGN = 2 * NL
MOE_ROWS = NASSIGN + NE * MOE_SUB
MOE_ITEMS = MOE_ROWS // (MOE_SUB * MOE_NSUB) + NE
ROWMOVE_RB = 1024


def _cparams(sem, block_bytes, scratch_bytes=0):
    est = 2 * block_bytes + scratch_bytes
    limit = min(VMEM_CAP, max(32 * 1024 * 1024, int(est * 1.3) + 12 * 1024 * 1024))
    return pltpu.CompilerParams(dimension_semantics=sem, vmem_limit_bytes=limit)


def _nbytes(shape, dtype):
    n = 1
    for s in shape:
        n *= s
    return n * jnp.dtype(dtype).itemsize


def _seg(i):
    return jnp.minimum(i // SEG_TILES, 2)


def _split2(x):
    hi = x.astype(BF16)
    lo = (x - hi.astype(F32)).astype(BF16)
    return hi, lo


def _split3(x):
    hi = x.astype(BF16)
    r = x - hi.astype(F32)
    mid = r.astype(BF16)
    lo = (r - mid.astype(F32)).astype(BF16)
    return hi, mid, lo


def _dot(a, b):
    return jnp.dot(a, b, preferred_element_type=F32)


def _dot_nt(a, b):
    return lax.dot_general(a, b, (((1,), (1,)), ((), ())), preferred_element_type=F32)


def _dot_tn(a, b):
    return lax.dot_general(a, b, (((0,), (0,)), ((), ())), preferred_element_type=F32)


def _dot3(a, b):
    ah, al = _split2(a)
    bh, bl = _split2(b)
    return _dot(ah, bh) + _dot(al, bh) + _dot(ah, bl)


def _dot_exact_lhs(m_bf16, x):
    h, m, l = _split3(x)
    return _dot(m_bf16, h) + _dot(m_bf16, m) + _dot(m_bf16, l)


def _group_sum(x, bo):
    outs = []
    for j in range(x.shape[-1] // LANES):
        h, l = _split2(x[:, j * LANES:(j + 1) * LANES])
        outs.append(_dot(h, bo) + _dot(l, bo))
    return outs[0] if len(outs) == 1 else jnp.concatenate(outs, axis=-1)


def _sigmoid(x):
    return 1.0 / (1.0 + jnp.exp(-x))


def _silu(x):
    return x * _sigmoid(x)


def _softplus(x):
    return jnp.maximum(x, 0.0) + jnp.log1p(jnp.exp(-jnp.abs(x)))


def _rms_mod(x, gain, shift, scale):
    ms = jnp.mean(x * x, axis=-1, keepdims=True)
    y = x * lax.rsqrt(ms + NORM_EPS) * gain
    return y * (1.0 + scale) + shift


def _adaln_kernel(c_ref, w_ref, b_ref, o_ref):
    a = _silu(c_ref[...])
    o_ref[...] = _dot3(a, w_ref[...]) + b_ref[...]


def adaln(cond8, w_mod, b_mod):
    n = w_mod.shape[1]
    tn = 1024
    return pl.pallas_call(
        _adaln_kernel,
        grid=(n // tn,),
        in_specs=[pl.BlockSpec((8, D), lambda j: (0, 0)),
                  pl.BlockSpec((D, tn), lambda j: (0, j)),
                  pl.BlockSpec((1, tn), lambda j: (0, j))],
        out_specs=pl.BlockSpec((8, tn), lambda j: (0, j)),
        out_shape=jax.ShapeDtypeStruct((8, n), F32),
        compiler_params=_cparams(("parallel",), _nbytes((D, tn), F32), 3 * _nbytes((D, tn), F32)),
    )(cond8, w_mod, b_mod.reshape(1, n))


def _mods(cond8, w_mod, b_mod):
    m = adaln(cond8, w_mod, b_mod)[:3].reshape(3, 6, 1, D)
    return [m[:, i] for i in range(6)]


def _normmod_kernel(x_ref, g_ref, sh_ref, sc_ref, o_ref):
    o_ref[...] = _rms_mod(x_ref[...], g_ref[...], sh_ref[...], sc_ref[...]).astype(BF16)


def norm_mod(x, gain, shift, scale):
    n = x.shape[0]
    vec = pl.BlockSpec((None, 1, D), lambda i: (_seg(i), 0, 0))
    return pl.pallas_call(
        _normmod_kernel,
        grid=(n // TM,),
        in_specs=[pl.BlockSpec((TM, D), lambda i: (i, 0)),
                  pl.BlockSpec((1, D), lambda i: (0, 0)), vec, vec],
        out_specs=pl.BlockSpec((TM, D), lambda i: (i, 0)),
        out_shape=jax.ShapeDtypeStruct((n, D), BF16),
        compiler_params=_cparams(("parallel",), _nbytes((TM, D), F32) * 2),
    )(x, gain.reshape(1, D), shift, scale)


def _mm_kernel(a_ref, w_ref, o_ref):
    o_ref[...] = _dot(a_ref[...], w_ref[...]).astype(o_ref.dtype)


def matmul(a, w, out_dtype, tm, tn):
    m, k = a.shape
    n = w.shape[1]
    blk = _nbytes((tm, k), a.dtype) + _nbytes((k, tn), w.dtype) + _nbytes((tm, tn), out_dtype)
    return pl.pallas_call(
        _mm_kernel,
        grid=(m // tm, n // tn),
        in_specs=[pl.BlockSpec((tm, k), lambda i, j: (i, 0)),
                  pl.BlockSpec((k, tn), lambda i, j: (0, j))],
        out_specs=pl.BlockSpec((tm, tn), lambda i, j: (i, j)),
        out_shape=jax.ShapeDtypeStruct((m, n), out_dtype),
        compiler_params=_cparams(("parallel", "parallel"), blk, _nbytes((tm, tn), F32)),
    )(a, w)


def _rwkv_prep_kernel(x_ref, xp_ref, xn_ref, mu_ref, wd_ref, wa_ref, wg_ref, w0_ref, a0_ref,
                      kk_ref, ka_ref, rk_ref, bo_ref,
                      r_out, v_out, nkk_out, g_out, bonus_out, w_out, kd_out, z_out):
    i = pl.program_id(0)
    lat_tiles = NL // PREP_TM
    seq_tiles = T // PREP_TM
    is_ctx = i >= lat_tiles
    first = jnp.logical_or(is_ctx, i % seq_tiles == 0)
    last = jnp.logical_or(is_ctx, i % seq_tiles == seq_tiles - 1)
    x = x_ref[...]
    prev_row = jnp.where(first, 0.0, xp_ref[7:8, :])
    next_row = jnp.where(last, 0.0, xn_ref[0:1, :])
    rows = lax.broadcasted_iota(I32, x.shape, 0)
    xp = jnp.where(rows == 0, prev_row, pltpu.roll(x, 1, axis=0))
    xn = jnp.where(rows == PREP_TM - 1, next_row, pltpu.roll(x, PREP_TM - 1, axis=0))
    ps = x + mu_ref[...] * (0.5 * (xp + xn) - x)

    bo = bo_ref[...]
    r = ps[:, 0:RW]
    k = ps[:, RW:2 * RW]
    v = ps[:, 2 * RW:3 * RW]
    dec_in = ps[:, 3 * RW:3 * RW + 128]
    icl_in = ps[:, 3 * RW + 128:3 * RW + 256]
    gate_in = ps[:, 3 * RW + 256:3 * RW + 384]
    lora_dec = _dot(jnp.tanh(dec_in).astype(BF16), wd_ref[...])
    lora_icl = _dot(icl_in.astype(BF16), wa_ref[...])
    g = _dot(_sigmoid(gate_in).astype(BF16), wg_ref[...])

    kk = k * kk_ref[...]
    kkn = kk * lax.rsqrt(_group_sum(kk * kk, bo) + 1e-12)
    ksum = jnp.zeros_like(k)
    for d in range(2):
        sl = slice(d * RW, (d + 1) * RW)
        w_log = -_softplus(-(w0_ref[:, sl] + lora_dec[:, sl])) - 0.5
        decay = jnp.exp(-jnp.exp(w_log))
        a = _sigmoid(a0_ref[:, sl] + lora_icl[:, sl])
        k_d = k * (1.0 + (a - 1.0) * ka_ref[...])
        ksum = ksum + k_d
        w_out[d] = decay
        kd_out[d] = k_d
        z_out[d] = kkn * a
    r_out[...] = r
    v_out[...] = v
    nkk_out[...] = -kkn
    g_out[...] = g
    bonus_out[...] = _group_sum(r * (0.5 * ksum) * rk_ref[...], bo) * v


def rwkv_prep(pr, mu, wd_blk, wa_blk, wg, w0, a0, k_k, k_a, r_k, bo):
    n = pr.shape[0]
    nt = n // PREP_TM
    cols = pr.shape[1]
    row = lambda c: pl.BlockSpec((1, c), lambda i: (0, 0))
    full = lambda s: pl.BlockSpec(s, lambda i: (0, 0))
    o1 = pl.BlockSpec((PREP_TM, RW), lambda i: (i, 0))
    o2 = pl.BlockSpec((2, PREP_TM, RW), lambda i: (0, i, 0))
    s1 = jax.ShapeDtypeStruct((n, RW), F32)
    s2 = jax.ShapeDtypeStruct((2, n, RW), F32)
    blk = _nbytes((PREP_TM, cols), F32) + 11 * _nbytes((PREP_TM, RW), F32)
    return pl.pallas_call(
        _rwkv_prep_kernel,
        grid=(nt,),
        in_specs=[pl.BlockSpec((PREP_TM, cols), lambda i: (i, 0)),
                  pl.BlockSpec((8, cols), lambda i: (jnp.maximum(i * (PREP_TM // 8) - 1, 0), 0)),
                  pl.BlockSpec((8, cols), lambda i: (jnp.minimum((i + 1) * (PREP_TM // 8), n // 8 - 1), 0)),
                  row(cols), full((128, 2 * RW)), full((128, 2 * RW)), full((128, RW)),
                  row(2 * RW), row(2 * RW), row(RW), row(RW), row(RW), full((128, 128))],
        out_specs=[o1, o1, o1, o1, o1, o2, o2, o2],
        out_shape=[s1, s1, s1, s1, s1, s2, s2, s2],
        compiler_params=_cparams(("parallel",), blk, 12 * _nbytes((PREP_TM, cols), F32)),
    )(pr, pr, pr, mu, wd_blk, wa_blk, wg, w0, a0, k_k, k_a, r_k, bo)


def _rwkv_scan_kernel(rf_ref, vf_ref, nf_ref, wf_ref, kf_ref, zf_ref,
                      rb_ref, vb_ref, nb_ref, wb_ref, kb_ref, zb_ref,
                      s0_ref, bo_ref, m1_ref,
                      yf_ref, yb_ref, sfin_ref, s_scr, yt_scr):
    j = pl.program_id(1)
    nb = pl.num_programs(1)

    @pl.when(j == 0)
    def _():
        s_scr[...] = s0_ref[...]

    yt_scr[...] = jnp.zeros_like(yt_scr)
    bo = bo_ref[...]
    m1 = m1_ref[...]
    lane_mod = lax.broadcasted_iota(I32, (HD, LANES), 1) % HD
    dirs = ((rf_ref, vf_ref, nf_ref, wf_ref, kf_ref, zf_ref),
            (rb_ref, vb_ref, nb_ref, wb_ref, kb_ref, zb_ref))

    shape3 = (NPAIR, HD, LANES)

    def bsum(x):
        return _dot(x.reshape(NPAIR * HD, LANES).astype(BF16), bo).reshape(shape3)

    sub8 = 8
    ngroups = SCAN_TB // sub8

    def group(gi, carry):
        for u in range(sub8):
            for d in range(2):
                if d == 0:
                    base, off = gi * sub8, u
                else:
                    base, off = (ngroups - 1 - gi) * sub8, sub8 - 1 - u
                base = pl.multiple_of(base, sub8)
                t = base + off
                r_ref, v_ref, n_ref, w_ref, k_ref, z_ref = dirs[d]

                def row(ref):
                    parts = [ref[pl.ds(base, sub8), p * LANES:(p + 1) * LANES][off:off + 1].reshape(1, 1, LANES)
                             for p in range(NPAIR)]
                    return jnp.broadcast_to(jnp.concatenate(parts, axis=0), shape3)

                st = s_scr[d]
                sa = bsum(st * row(n_ref))
                vcol = bsum(m1 * row(v_ref))
                st = st * row(w_ref) + sa * row(z_ref) + vcol * row(k_ref)
                s_scr[d] = st
                ycol = bsum(st * row(r_ref))
                yt_scr[d] = jnp.where(lane_mod == t, ycol, yt_scr[d])
        return carry

    lax.fori_loop(0, ngroups, group, 0)

    lane = lax.broadcasted_iota(I32, (HD, LANES), 1)
    sub = lax.broadcasted_iota(I32, (HD, LANES), 0)
    pick = jnp.where(lane_mod == sub, 1.0, 0.0).astype(BF16)
    for d, y_ref in ((0, yf_ref), (1, yb_ref)):
        for p in range(NPAIR):
            yt = yt_scr[d, p]
            z = jnp.concatenate([jnp.where(lane < HD, yt, 0.0), jnp.where(lane >= HD, yt, 0.0)], axis=0)
            zh, zl = _split2(z)
            y_ref[:, p * LANES:(p + 1) * LANES] = _dot_nt(pick, zh) + _dot_nt(pick, zl)

    @pl.when(j == nb - 1)
    def _():
        sfin_ref[...] = s_scr[...]


def rwkv_scan(r, v, nkk, w, kd, z, s0, bo, m1, row0, seq):
    nb = seq // SCAN_TB
    base = row0 // SCAN_TB

    def fwd(b, j):
        return base + b * nb + j

    def bwd(b, j):
        return base + b * nb + nb - 1 - j

    def shared(idx):
        return pl.BlockSpec((SCAN_TB, RW), lambda b, j: (idx(b, j), 0))

    def per_dir(d, idx):
        return pl.BlockSpec((None, SCAN_TB, RW), lambda b, j: (d, idx(b, j), 0))

    st_spec = pl.BlockSpec((None, 2, NPAIR, HD, LANES), lambda b, j: (b, 0, 0, 0, 0))
    blk = 14 * _nbytes((SCAN_TB, RW), F32) + 2 * _nbytes((2, NPAIR, HD, LANES), F32)
    return pl.pallas_call(
        _rwkv_scan_kernel,
        grid=(B, nb),
        in_specs=[shared(fwd), shared(fwd), shared(fwd), per_dir(0, fwd), per_dir(0, fwd), per_dir(0, fwd),
                  shared(bwd), shared(bwd), shared(bwd), per_dir(1, bwd), per_dir(1, bwd), per_dir(1, bwd),
                  st_spec,
                  pl.BlockSpec((LANES, LANES), lambda b, j: (0, 0)),
                  pl.BlockSpec((HD, LANES), lambda b, j: (0, 0))],
        out_specs=[pl.BlockSpec((SCAN_TB, RW), lambda b, j: (b * nb + j, 0)),
                   pl.BlockSpec((SCAN_TB, RW), lambda b, j: (b * nb + nb - 1 - j, 0)),
                   st_spec],
        out_shape=[jax.ShapeDtypeStruct((B * seq, RW), F32),
                   jax.ShapeDtypeStruct((B * seq, RW), F32),
                   jax.ShapeDtypeStruct((B, 2, NPAIR, HD, LANES), F32)],
        scratch_shapes=[pltpu.VMEM((2, NPAIR, HD, LANES), F32),
                        pltpu.VMEM((2, NPAIR, HD, LANES), F32)],
        compiler_params=_cparams(("parallel", "arbitrary"), blk, 2 * _nbytes((2, NPAIR, HD, LANES), F32)),
    )(r, v, nkk, w, kd, z, r, v, nkk, w, kd, z, s0, bo, m1)


def _na_norm(x, gain, bo):
    xf = x.astype(F32)
    ms = _group_sum(xf * xf, bo) * (1.0 / HD)
    return xf * lax.rsqrt(ms + NORM_EPS) * gain


def _na_kernel(q_ref, k_ref, v_ref, qc_ref, kc_ref, vc_ref, bias_ref, qg_ref, kg_ref, bo_ref,
               o_ref, oc_ref, qn_scr, kn_scr):
    bo = bo_ref[...]
    scale = HD ** -0.5
    nchunk = 8
    rows = T // nchunk

    def norm_chunk(c, carry):
        sl = pl.ds(pl.multiple_of(c * rows, rows), rows)
        qn_scr[sl, :] = (_na_norm(q_ref[sl, :], qg_ref[...], bo) * scale).astype(BF16)
        kn_scr[sl, :] = _na_norm(k_ref[sl, :], kg_ref[...], bo).astype(BF16)
        return carry

    lax.fori_loop(0, nchunk, norm_chunk, 0)
    kc = _na_norm(kc_ref[...], kg_ref[...], bo).astype(BF16)
    qc = (_na_norm(qc_ref[...], qg_ref[...], bo) * scale).astype(BF16)
    vc = vc_ref[...]

    def two_heads(q2):
        lane = lax.broadcasted_iota(I32, q2.shape, 1)
        zero = jnp.zeros_like(q2)
        return jnp.concatenate([jnp.where(lane < HD, q2, zero), jnp.where(lane >= HD, q2, zero)], axis=0)

    def merge_heads(o, n):
        lane = lax.broadcasted_iota(I32, (n, LANES), 1)
        return jnp.where(lane < HD, o[0:n], o[n:2 * n])

    s = _dot_nt(two_heads(qc), kc)
    p = jnp.exp(s - jnp.max(s, axis=-1, keepdims=True))
    o = _dot(p.astype(BF16), vc) / jnp.sum(p, axis=-1, keepdims=True)
    oc_ref[...] = merge_heads(o, C).astype(BF16)

    band = NA_KH * GRID_W

    def row_step(r, carry):
        r0 = jnp.clip(r - NA_KH // 2, 0, GRID_H - NA_KH)
        dr = r - r0
        q2 = qn_scr[pl.ds(pl.multiple_of(r * GRID_W, GRID_W), GRID_W), :]
        ksl = pl.ds(pl.multiple_of(r0 * GRID_W, GRID_W), band)
        lhs = two_heads(q2)
        bias = jnp.concatenate([bias_ref[0, dr], bias_ref[1, dr]], axis=0)
        s_w = _dot_nt(lhs, kn_scr[ksl, :]) + bias
        s_c = _dot_nt(lhs, kc)
        m = jnp.maximum(jnp.max(s_w, axis=-1, keepdims=True), jnp.max(s_c, axis=-1, keepdims=True))
        p_w = jnp.exp(s_w - m)
        p_c = jnp.exp(s_c - m)
        l = jnp.sum(p_w, axis=-1, keepdims=True) + jnp.sum(p_c, axis=-1, keepdims=True)
        o = (_dot(p_w.astype(BF16), v_ref[ksl, :]) + _dot(p_c.astype(BF16), vc)) / l
        o_ref[pl.ds(pl.multiple_of(r * GRID_W, GRID_W), GRID_W), :] = merge_heads(o, GRID_W).astype(BF16)
        return carry

    lax.fori_loop(0, GRID_H, row_step, 0)


def na_attention(pna, bias_tab, qg, kg, bo):
    hp = NA_W // LANES
    ctx0 = NL // C

    def lat(col0):
        return pl.BlockSpec((T, LANES), lambda b, p: (b, col0 + p))

    def ctx(col0):
        return pl.BlockSpec((C, LANES), lambda b, p: (ctx0 + b, col0 + p))

    vec = pl.BlockSpec((1, LANES), lambda b, p: (0, 0))
    blk = 3 * _nbytes((T, LANES), BF16) + _nbytes((2, NA_KH, GRID_W, NA_KH * GRID_W), F32) + _nbytes((T, LANES), BF16)
    return pl.pallas_call(
        _na_kernel,
        grid=(B, hp),
        in_specs=[lat(0), lat(hp), lat(2 * hp), ctx(0), ctx(hp), ctx(2 * hp),
                  pl.BlockSpec((2, NA_KH, GRID_W, NA_KH * GRID_W), lambda b, p: (p, 0, 0, 0)),
                  vec, vec, pl.BlockSpec((LANES, LANES), lambda b, p: (0, 0))],
        out_specs=[pl.BlockSpec((T, LANES), lambda b, p: (b, p)),
                   pl.BlockSpec((C, LANES), lambda b, p: (b, p))],
        out_shape=[jax.ShapeDtypeStruct((NL, NA_W), BF16), jax.ShapeDtypeStruct((NC, NA_W), BF16)],
        scratch_shapes=[pltpu.VMEM((T, LANES), BF16), pltpu.VMEM((T, LANES), BF16)],
        compiler_params=_cparams(("parallel", "parallel"), blk, 2 * _nbytes((T, LANES), BF16)),
    )(pna, pna, pna, pna, pna, pna, bias_tab, qg, kg, bo)


def _na_bias_table(rpb):
    q = jnp.arange(GRID_W)[:, None]
    kc = jnp.arange(GRID_W)[None, :]
    cs = jnp.clip(q - NA_KW // 2, 0, GRID_W - NA_KW)
    valid = (kc >= cs) & (kc < cs + NA_KW)
    cbi = jnp.clip(kc - q + NA_KW - 1, 0, 2 * NA_KW - 2)
    dr = jnp.arange(NA_KH)[:, None]
    bi = jnp.arange(NA_KH)[None, :]
    rbi = bi - dr + NA_KH - 1
    tab = rpb[:, rbi[:, :, None, None], cbi[None, None, :, :]]
    tab = jnp.where(valid[None, None, None], tab, -1e30)
    return tab.transpose(0, 1, 3, 2, 4).reshape(rpb.shape[0], NA_KH, GRID_W, NA_KH * GRID_W)


def _l0_out_kernel(yf_ref, yb_ref, bonus_ref, g_ref, bna_ref, x_ref, wa_ref, wb_ref,
                   lng_ref, lnb_ref, gt_ref, n2_ref, sh_ref, sc_ref, bo_ref, xo_ref, ho_ref):
    bo = bo_ref[...]
    y = yf_ref[...] + yb_ref[...]
    mu = _group_sum(y, bo) * (1.0 / HD)
    dlt = y - mu
    var = _group_sum(dlt * dlt, bo) * (1.0 / HD)
    yn = dlt * lax.rsqrt(var + RWKV_GN_EPS) * lng_ref[...] + lnb_ref[...]
    a = ((yn + bonus_ref[...]) * g_ref[...]).astype(BF16)
    acc = _dot(a, wa_ref[...]) + _dot(bna_ref[...], wb_ref[...])
    xo = x_ref[...] + gt_ref[...] * acc
    xo_ref[...] = xo
    ho_ref[...] = _rms_mod(xo, n2_ref[...], sh_ref[...], sc_ref[...]).astype(BF16)


def l0_out(yf, yb, bonus, g, bna, x, wa, wb, lnx_g, lnx_b, gt, norm2, sh, sc, bo):
    n = x.shape[0]
    half = pl.BlockSpec((TM, RW), lambda i: (i, 0))
    fullrow = pl.BlockSpec((TM, D), lambda i: (i, 0))
    wspec = pl.BlockSpec((RW, D), lambda i: (0, 0))
    vec = pl.BlockSpec((None, 1, D), lambda i: (_seg(i), 0, 0))
    r1 = pl.BlockSpec((1, RW), lambda i: (0, 0))
    blk = (4 * _nbytes((TM, RW), F32) + _nbytes((TM, RW), BF16) + 2 * _nbytes((TM, D), F32)
           + _nbytes((TM, D), BF16) + 2 * _nbytes((RW, D), BF16))
    return pl.pallas_call(
        _l0_out_kernel,
        grid=(n // TM,),
        in_specs=[half, half, half, half, half, fullrow, wspec, wspec, r1, r1, vec,
                  pl.BlockSpec((1, D), lambda i: (0, 0)), vec, vec,
                  pl.BlockSpec((LANES, LANES), lambda i: (0, 0))],
        out_specs=[fullrow, fullrow],
        out_shape=[jax.ShapeDtypeStruct((n, D), F32), jax.ShapeDtypeStruct((n, D), BF16)],
        compiler_params=_cparams(("parallel",), blk, 4 * _nbytes((TM, D), F32)),
    )(yf, yb, bonus, g, bna, x, wa, wb, lnx_g, lnx_b, gt, norm2, sh, sc, bo)


FFN_TF = 512


def _ffn_kernel(h_ref, x_ref, wg_ref, wu_ref, wd_ref, gt_ref, n_ref, sh_ref, sc_ref, xo_ref, ho_ref, acc):
    f = pl.program_id(1)

    @pl.when(f == 0)
    def _():
        acc[...] = jnp.zeros_like(acc)

    h = h_ref[...]
    act = (_silu(_dot(h, wg_ref[...])) * _dot(h, wu_ref[...])).astype(BF16)
    acc[...] += _dot(act, wd_ref[...])

    @pl.when(f == pl.num_programs(1) - 1)
    def _():
        xo = x_ref[...] + gt_ref[...] * acc[...]
        xo_ref[...] = xo
        ho_ref[...] = _rms_mod(xo, n_ref[...], sh_ref[...], sc_ref[...]).astype(BF16)


def ffn(h, x, wg, wu, wd, gt, norm_next, sh_next, sc_next):
    n = x.shape[0]
    rowblk = pl.BlockSpec((TM, D), lambda i, f: (i, 0))
    vec = pl.BlockSpec((None, 1, D), lambda i, f: (_seg(i), 0, 0))
    blk = (_nbytes((TM, D), BF16) * 2 + _nbytes((TM, D), F32) * 2
           + 3 * _nbytes((D, FFN_TF), BF16))
    return pl.pallas_call(
        _ffn_kernel,
        grid=(n // TM, FFN // FFN_TF),
        in_specs=[rowblk, rowblk,
                  pl.BlockSpec((D, FFN_TF), lambda i, f: (0, f)),
                  pl.BlockSpec((D, FFN_TF), lambda i, f: (0, f)),
                  pl.BlockSpec((FFN_TF, D), lambda i, f: (f, 0)),
                  vec, pl.BlockSpec((1, D), lambda i, f: (0, 0)), vec, vec],
        out_specs=[rowblk, rowblk],
        out_shape=[jax.ShapeDtypeStruct((n, D), F32), jax.ShapeDtypeStruct((n, D), BF16)],
        scratch_shapes=[pltpu.VMEM((TM, D), F32)],
        compiler_params=_cparams(("parallel", "arbitrary"), blk, 3 * _nbytes((TM, D), F32)),
    )(h, x, wg, wu, wd, gt, norm_next, sh_next, sc_next)


def _gla_kernel(k_ref, v_ref, q_ref, lr_ref, w2_ref, gb_ref, s0_ref, o_ref, sfin_ref, s_scr, *, rev, nchunks):
    j = pl.program_id(2)

    @pl.when(j == 0)
    def _():
        s_scr[...] = s0_ref[...]

    L = GLA_CHUNK
    row = lax.broadcasted_iota(I32, (L, L), 0)
    col = lax.broadcasted_iota(I32, (L, L), 1)
    tri = jnp.where((col >= row) if rev else (col <= row), 1.0, 0.0).astype(BF16)
    nsb = L // GLA_SUB

    def chunk(ci, carry):
        c = nchunks - 1 - ci if rev else ci
        sl = pl.ds(pl.multiple_of(c * L, L), L)
        k = k_ref[sl, :].astype(F32)
        v = v_ref[sl, :]
        q = q_ref[sl, :].astype(F32) * (GLA_DK ** -0.5)
        pre = _dot3(lr_ref[sl, :], w2_ref[...]) + gb_ref[...]
        g = (jnp.minimum(pre, 0.0) - jnp.log1p(jnp.exp(-jnp.abs(pre)))) * (1.0 / GLA_NORMALIZER)
        b = _dot_exact_lhs(tri, g)
        bend = b[0:1] if rev else b[L - 1:L]
        st = s_scr[...]
        o_inter = _dot_nt((q * jnp.exp(b)).astype(BF16), st.astype(BF16))
        blocks = []
        for i in range(nsb):
            if rev:
                r_lo, r_hi = L - GLA_SUB * (i + 1), L - GLA_SUB * i
                k_lo, k_hi = r_lo, L
                ref = b[r_hi:r_hi + 1] if i > 0 else None
            else:
                r_lo, r_hi = GLA_SUB * i, GLA_SUB * (i + 1)
                k_lo, k_hi = 0, r_hi
                ref = b[r_lo - 1:r_lo] if i > 0 else None
            bq = b[r_lo:r_hi]
            bk = b[k_lo:k_hi]
            if ref is None:
                qe = q[r_lo:r_hi] * jnp.exp(bq)
                ke = k[k_lo:k_hi] * jnp.exp(-bk)
            else:
                qe = q[r_lo:r_hi] * jnp.exp(bq - ref)
                ke = k[k_lo:k_hi] * jnp.exp(ref - bk)
            att = _dot_nt(qe.astype(BF16), ke.astype(BF16))
            nk = k_hi - k_lo
            tq = lax.broadcasted_iota(I32, (GLA_SUB, nk), 0)
            sk = lax.broadcasted_iota(I32, (GLA_SUB, nk), 1)
            ok = (sk >= tq) if rev else (sk <= tq + r_lo)
            att = jnp.where(ok, att, 0.0)
            blocks.append((r_lo, _dot(att.astype(BF16), v[k_lo:k_hi])))
        blocks.sort(key=lambda t: t[0])
        o_ref[sl, :] = o_inter + jnp.concatenate([blk for _, blk in blocks], axis=0)
        kd = (k * jnp.exp(bend - b)).astype(BF16)
        s_scr[...] = st * jnp.exp(bend) + _dot_tn(v, kd)
        return carry

    lax.fori_loop(0, nchunks, chunk, 0)

    @pl.when(j == pl.num_programs(2) - 1)
    def _():
        sfin_ref[...] = s_scr[...]


def gla_scan(p1, lr, w2p, gbp, s0, d, row0, seq, tb):
    rev = d == 1
    nb = seq // tb
    base = row0 // tb
    nchunks = tb // GLA_CHUNK

    def blkidx(b, j):
        return base + b * nb + (nb - 1 - j if rev else j)

    kcol = GLA_KW // GLA_DK
    st_spec = pl.BlockSpec((None, None, GLA_DV, GLA_DK), lambda b, h, j: (b, h, 0, 0))
    blk = (2 * _nbytes((tb, GLA_DK), BF16) + _nbytes((tb, GLA_DV), BF16) + _nbytes((tb, LANES), F32)
           + _nbytes((tb, GLA_DV), F32) + 2 * _nbytes((GLA_DV, GLA_DK), F32))
    return pl.pallas_call(
        functools.partial(_gla_kernel, rev=rev, nchunks=nchunks),
        grid=(B, GLA_H, nb),
        in_specs=[pl.BlockSpec((tb, GLA_DK), lambda b, h, j: (blkidx(b, j), h)),
                  pl.BlockSpec((tb, GLA_DV), lambda b, h, j: (blkidx(b, j), GLA_KW // GLA_DV + h)),
                  pl.BlockSpec((tb, GLA_DK), lambda b, h, j: (blkidx(b, j), (GLA_KW + GLA_VW) // GLA_DK + h)),
                  pl.BlockSpec((tb, LANES), lambda b, h, j: (blkidx(b, j), 0)),
                  pl.BlockSpec((None, LANES, GLA_DK), lambda b, h, j: (d, 0, h)),
                  pl.BlockSpec((None, 1, GLA_DK), lambda b, h, j: (d, 0, h)),
                  st_spec],
        out_specs=[pl.BlockSpec((tb, GLA_DV), lambda b, h, j: (b * nb + (nb - 1 - j if rev else j), h)),
                   st_spec],
        out_shape=[jax.ShapeDtypeStruct((B * seq, GLA_VW), F32),
                   jax.ShapeDtypeStruct((B, GLA_H, GLA_DV, GLA_DK), F32)],
        scratch_shapes=[pltpu.VMEM((GLA_DV, GLA_DK), F32)],
        compiler_params=_cparams(("parallel", "parallel", "arbitrary"), blk, 8 * _nbytes((GLA_DV, GLA_DK), F32)),
    )(p1, p1, p1, lr, w2p, gbp, s0)


def _l1_out_kernel(of_ref, ob_ref, gate_ref, x_ref, w_ref, on_ref, gt_ref, n2_ref, sh_ref, sc_ref, rt_ref,
                   xo_ref, hu_ref, route_ref):
    o = of_ref[...] + ob_ref[...]
    heads = []
    for h in range(GLA_H):
        oh = o[:, h * GLA_DV:(h + 1) * GLA_DV]
        ms = jnp.mean(oh * oh, axis=-1, keepdims=True)
        heads.append(oh * lax.rsqrt(ms + NORM_EPS) * on_ref[...])
    og = (jnp.concatenate(heads, axis=-1) * _silu(gate_ref[...].astype(F32))).astype(BF16)
    xo = x_ref[...] + gt_ref[...] * _dot(og, w_ref[...])
    xo_ref[...] = xo
    h4 = _rms_mod(xo, n2_ref[...], sh_ref[...], sc_ref[...])

    logits = _dot3(h4, rt_ref[...])
    lane = lax.broadcasted_iota(I32, logits.shape, 1)
    neg = -jnp.inf
    lg = jnp.where(lane < NE, logits, neg)
    m1 = jnp.max(lg, axis=-1, keepdims=True)
    i1 = jnp.min(jnp.where(lg == m1, lane, LANES), axis=-1, keepdims=True)
    lg2 = jnp.where(lane == i1, neg, lg)
    m2 = jnp.max(lg2, axis=-1, keepdims=True)
    i2 = jnp.min(jnp.where(lg2 == m2, lane, LANES), axis=-1, keepdims=True)
    e2 = jnp.exp(m2 - m1)
    w1 = 1.0 / (1.0 + e2)
    w2 = e2 / (1.0 + e2)
    route = jnp.where(lane == 0, i1.astype(F32),
                      jnp.where(lane == 1, i2.astype(F32),
                                jnp.where(lane == 2, w1, jnp.where(lane == 3, w2, 0.0))))
    route_ref[...] = route

    hb = lax.bitcast_convert_type(h4.astype(BF16).astype(F32), U32)
    hu_ref[...] = (hb[:, :D // 2] >> 16) | (hb[:, D // 2:] & jnp.uint32(0xFFFF0000))


def l1_out(of, ob, p1, x, w_out, o_norm, gt, norm2, sh, sc, router_p):
    n = NL
    tm = K10_TM
    seg = lambda i: i // (T // tm)
    rowblk = pl.BlockSpec((tm, D), lambda i: (i, 0))
    vec = pl.BlockSpec((None, 1, D), lambda i: (seg(i), 0, 0))
    blk = (3 * _nbytes((tm, D), F32) + _nbytes((tm, D), BF16) + _nbytes((D, D), BF16)
           + _nbytes((D, LANES), F32) + _nbytes((tm, D), F32) + _nbytes((tm, D // 2), U32))
    return pl.pallas_call(
        _l1_out_kernel,
        grid=(n // tm,),
        in_specs=[rowblk, rowblk,
                  pl.BlockSpec((tm, GLA_VW), lambda i: (i, (2 * GLA_KW + GLA_VW) // GLA_VW)),
                  rowblk, pl.BlockSpec((D, D), lambda i: (0, 0)),
                  pl.BlockSpec((1, GLA_DV), lambda i: (0, 0)), vec,
                  pl.BlockSpec((1, D), lambda i: (0, 0)), vec, vec,
                  pl.BlockSpec((D, LANES), lambda i: (0, 0))],
        out_specs=[rowblk, pl.BlockSpec((tm, D // 2), lambda i: (i, 0)),
                   pl.BlockSpec((tm, LANES), lambda i: (i, 0))],
        out_shape=[jax.ShapeDtypeStruct((n, D), F32), jax.ShapeDtypeStruct((n, D // 2), U32),
                   jax.ShapeDtypeStruct((n, LANES), F32)],
        compiler_params=_cparams(("parallel",), blk, 4 * _nbytes((tm, D), F32)),
    )(of, ob, p1, x, w_out, o_norm, gt, norm2, sh, sc, router_p)


def _rowmove_kernel(sidx_ref, didx_ref, src_ref, *rest):
    dst_ref, sem = rest[-2], rest[-1]
    base = pl.program_id(0) * ROWMOVE_RB

    def copy(s, d):
        return pltpu.make_async_copy(src_ref.at[pl.ds(s, 1)], dst_ref.at[pl.ds(d, 1)], sem)

    def issue(i, carry):
        copy(sidx_ref[base + i], didx_ref[base + i]).start()
        return carry

    def drain(i, carry):
        copy(sidx_ref[base + i], didx_ref[base + i]).wait()
        return carry

    lax.fori_loop(0, ROWMOVE_RB, issue, 0)
    lax.fori_loop(0, ROWMOVE_RB, drain, 0)


def row_move(sidx, didx, src, dst_rows, dst_init=None):
    n = sidx.shape[0]
    anyspec = pl.BlockSpec(memory_space=pl.ANY)
    ins = [src] if dst_init is None else [src, dst_init]
    return pl.pallas_call(
        _rowmove_kernel,
        grid_spec=pltpu.PrefetchScalarGridSpec(
            num_scalar_prefetch=2,
            grid=(n // ROWMOVE_RB,),
            in_specs=[anyspec] * len(ins),
            out_specs=anyspec,
            scratch_shapes=[pltpu.SemaphoreType.DMA(())]),
        out_shape=jax.ShapeDtypeStruct((dst_rows, src.shape[1]), src.dtype),
        input_output_aliases={} if dst_init is None else {3: 0},
        compiler_params=pltpu.CompilerParams(dimension_semantics=("arbitrary",)),
    )(sidx, didx, *ins)


def _moe_kernel(ie_ref, is_ref, in_ref, nit_ref, *refs):
    x_refs = refs[0:MOE_NSUB]
    wg_ref, wu_ref, wd_ref, ys_ref, xb, acc, sem = refs[MOE_NSUB:]
    i = pl.program_id(0)
    f = pl.program_id(1)
    valid = i < nit_ref[0]

    @pl.when(jnp.logical_and(valid, f == 0))
    def _():
        for s in range(MOE_NSUB):
            w = x_refs[s][...]
            lo = lax.bitcast_convert_type(w << 16, F32)
            hi = lax.bitcast_convert_type(w & jnp.uint32(0xFFFF0000), F32)
            xb[s * MOE_SUB:(s + 1) * MOE_SUB, :] = jnp.concatenate([lo, hi], axis=-1).astype(BF16)
        acc[...] = jnp.zeros_like(acc)

    @pl.when(valid)
    def _():
        x = xb[...]
        act = (_silu(_dot(x, wg_ref[...].astype(BF16))) * _dot(x, wu_ref[...].astype(BF16))).astype(BF16)
        acc[...] += _dot(act, wd_ref[...].astype(BF16))

    @pl.when(jnp.logical_and(valid, f == pl.num_programs(1) - 1))
    def _():
        start = is_ref[i]
        nsub = in_ref[i]
        for s in range(MOE_NSUB):
            @pl.when(s < nsub)
            def _():
                dst = ys_ref.at[pl.ds(pl.multiple_of((start + s) * MOE_SUB, MOE_SUB), MOE_SUB)]
                cp = pltpu.make_async_copy(acc.at[pl.ds(s * MOE_SUB, MOE_SUB)], dst, sem)
                cp.start()
                cp.wait()

    @pl.when(jnp.logical_and(i == pl.num_programs(0) - 1, f == pl.num_programs(1) - 1))
    def _():
        used = nit_ref[1]
        acc[0:MOE_SUB, :] = jnp.zeros((MOE_SUB, D), F32)
        for u in range(NE):
            @pl.when(used + u < MOE_ROWS // MOE_SUB)
            def _():
                dst = ys_ref.at[pl.ds(pl.multiple_of((used + u) * MOE_SUB, MOE_SUB), MOE_SUB)]
                cp = pltpu.make_async_copy(acc.at[pl.ds(0, MOE_SUB)], dst, sem)
                cp.start()
                cp.wait()


def moe_ffn(item_e, item_s, item_n, n_items, xs, wg, wu, wd):
    nf = EDIM // MOE_TF
    rows = MOE_SUB * MOE_NSUB

    def item(i, nit):
        return jnp.minimum(i, nit[0] - 1)

    def sub_idx(s):
        def idx(i, f, ie, ist, inn, nit):
            ii = item(i, nit)
            return (ist[ii] + jnp.minimum(s, inn[ii] - 1), 0)
        return idx

    x_specs = [pl.BlockSpec((MOE_SUB, D // 2), sub_idx(s)) for s in range(MOE_NSUB)]
    w_in = lambda i, f, ie, ist, inn, nit: (ie[item(i, nit)], 0, f)
    w_dn = lambda i, f, ie, ist, inn, nit: (ie[item(i, nit)], f, 0)
    blk = MOE_NSUB * _nbytes((MOE_SUB, D // 2), U32) + 3 * _nbytes((D, MOE_TF), F32)
    scratch = (_nbytes((rows, D), BF16) + _nbytes((rows, D), F32) + 6 * _nbytes((D, MOE_TF), BF16)
               + 3 * _nbytes((rows, MOE_TF), F32))
    return pl.pallas_call(
        _moe_kernel,
        grid_spec=pltpu.PrefetchScalarGridSpec(
            num_scalar_prefetch=4,
            grid=(MOE_ITEMS, nf),
            in_specs=x_specs + [pl.BlockSpec((None, D, MOE_TF), w_in),
                                pl.BlockSpec((None, D, MOE_TF), w_in),
                                pl.BlockSpec((None, MOE_TF, D), w_dn)],
            out_specs=pl.BlockSpec(memory_space=pl.ANY),
            scratch_shapes=[pltpu.VMEM((rows, D), BF16), pltpu.VMEM((rows, D), F32),
                            pltpu.SemaphoreType.DMA(())]),
        out_shape=jax.ShapeDtypeStruct((MOE_ROWS, D), F32),
        compiler_params=_cparams(("arbitrary", "arbitrary"), blk, scratch),
    )(item_e, item_s, item_n, n_items, *([xs] * MOE_NSUB), wg, wu, wd)


def _combine_kernel(x_ref, g0_ref, g1_ref, route_ref, gt_ref, o_ref):
    rt = route_ref[...]
    y = rt[:, 2:3] * g0_ref[...] + rt[:, 3:4] * g1_ref[...]
    o_ref[...] = x_ref[...] + gt_ref[...] * y


def moe_combine(x, g, route, gt):
    n = NL
    rowblk = pl.BlockSpec((TM, D), lambda i: (i, 0))
    return pl.pallas_call(
        _combine_kernel,
        grid=(n // TM,),
        in_specs=[rowblk, rowblk, pl.BlockSpec((TM, D), lambda i: (i + n // TM, 0)),
                  pl.BlockSpec((TM, LANES), lambda i: (i, 0)),
                  pl.BlockSpec((None, 1, D), lambda i: (i // SEG_TILES, 0, 0))],
        out_specs=rowblk,
        out_shape=jax.ShapeDtypeStruct((n, D), F32),
        compiler_params=_cparams(("parallel",), 4 * _nbytes((TM, D), F32)),
    )(x, g, g, route, gt)


def _block_ones():
    i = jnp.arange(LANES)
    return (i[:, None] // HD == i[None, :] // HD).astype(BF16)


def _diag_pick():
    return (jnp.arange(LANES)[None, :] % HD == jnp.arange(HD)[:, None]).astype(F32)


def _blockdiag2(w):
    z = jnp.zeros_like(w[0])
    return jnp.concatenate([jnp.concatenate([w[0], z], axis=1), jnp.concatenate([z, w[1]], axis=1)], axis=0)


def _route_tables(route):
    e = jnp.concatenate([route[:, 0], route[:, 1]]).astype(I32)
    onehot = (e[:, None] == jnp.arange(NE, dtype=I32)[None, :]).astype(I32)
    cum = jnp.cumsum(onehot, axis=0)
    rank = jnp.sum(onehot * cum, axis=1) - 1
    counts = cum[-1]
    nsub = (counts + MOE_SUB - 1) // MOE_SUB
    sub_start = jnp.cumsum(nsub) - nsub
    pos = (sub_start * MOE_SUB)[e] + rank
    nitem_e = (nsub + MOE_NSUB - 1) // MOE_NSUB
    item_end = jnp.cumsum(nitem_e)
    item_start = item_end - nitem_e
    ii = jnp.arange(MOE_ITEMS, dtype=I32)
    item_e = jnp.minimum(jnp.sum((ii[:, None] >= item_end[None, :]).astype(I32), axis=1), NE - 1)
    local = ii - item_start[item_e]
    item_s = sub_start[item_e] + MOE_NSUB * local
    item_n = jnp.clip(nsub[item_e] - MOE_NSUB * local, 1, MOE_NSUB)
    n_items = jnp.stack([item_end[-1], jnp.sum(nsub)]).astype(I32)
    return pos.astype(I32), item_e.astype(I32), item_s.astype(I32), item_n.astype(I32), n_items


def kernel(x, c, ctx, c_ctx, l0_w_mod, l0_b_mod, l0_norm1, l0_norm2, l0_w_in, l0_mu_shift, l0_decay_w0, l0_decay_w2, l0_iclr_a0, l0_iclr_a2, l0_gate_g2, l0_k_k, l0_k_a, l0_r_k, l0_lnx_g, l0_lnx_b, l0_q_norm, l0_k_norm, l0_rpb, l0_w_out, l0_ffn_wg, l0_ffn_wu, l0_ffn_wd, l1_w_mod, l1_b_mod, l1_norm1, l1_norm2, l1_w_in, l1_gate_w2, l1_gate_b, l1_o_norm, l1_w_out, l1_router, l1_exp_wg, l1_exp_wu, l1_exp_wd):
    bo = _block_ones()
    m1 = _diag_pick()
    xr = jnp.concatenate([x.reshape(NL, D), ctx.reshape(NC, D)], axis=0)
    cond8 = jnp.zeros((8, D), F32).at[0:B].set(c).at[B].set(c_ctx)

    sh1, sc1, gt1, sh2, sc2, gt2 = _mods(cond8, l0_w_mod, l0_b_mod)
    l1sh1, l1sc1, l1gt1, l1sh2, l1sc2, l1gt2 = _mods(cond8, l1_w_mod, l1_b_mod)
    h = norm_mod(xr, l0_norm1, sh1, sc1)
    pr = matmul(h, l0_w_in[:, :RWKV_COLS].astype(BF16), F32, 1088, 1152)
    pna = matmul(h, l0_w_in[:, RWKV_COLS:].astype(BF16), BF16, 1088, 1024)

    r, v, nkk, g, bonus, w, kd, z = rwkv_prep(
        pr, l0_mu_shift.reshape(1, -1),
        _blockdiag2(l0_decay_w2).astype(BF16), _blockdiag2(l0_iclr_a2).astype(BF16), l0_gate_g2.astype(BF16),
        l0_decay_w0.reshape(1, -1), l0_iclr_a0.reshape(1, -1),
        l0_k_k.reshape(1, -1), l0_k_a.reshape(1, -1), l0_r_k.reshape(1, -1), bo)
    s_zero = jnp.zeros((B, 2, NPAIR, HD, LANES), F32)
    ycf, ycb, s_ctx = rwkv_scan(r, v, nkk, w, kd, z, s_zero, bo, m1, NL, C)
    ylf, ylb, _ = rwkv_scan(r, v, nkk, w, kd, z, s_ctx, bo, m1, 0, T)
    yf = jnp.concatenate([ylf, ycf], axis=0)
    yb = jnp.concatenate([ylb, ycb], axis=0)

    b_l, b_c = na_attention(pna, _na_bias_table(l0_rpb),
                            jnp.tile(l0_q_norm, 2).reshape(1, LANES), jnp.tile(l0_k_norm, 2).reshape(1, LANES), bo)
    bna = jnp.concatenate([b_l, b_c], axis=0)

    w_out0 = l0_w_out.astype(BF16)
    x1, h2 = l0_out(yf, yb, bonus, g, bna, xr, w_out0[:RW], w_out0[RW:],
                    l0_lnx_g.reshape(1, RW), l0_lnx_b.reshape(1, RW), gt1, l0_norm2.reshape(1, D), sh2, sc2, bo)
    x2, h3 = ffn(h2, x1, l0_ffn_wg.astype(BF16), l0_ffn_wu.astype(BF16), l0_ffn_wd.astype(BF16),
                 gt2, l1_norm1.reshape(1, D), l1sh1, l1sc1)

    st = GLA_KW + GLA_VW
    w1 = jnp.concatenate([l1_w_in[:, :st], l1_w_in[:, st + 2 * GLA_RANK:]], axis=1).astype(BF16)
    wlr = jnp.pad(l1_w_in[:, st:st + 2 * GLA_RANK], ((0, 0), (0, LANES - 2 * GLA_RANK))).astype(BF16)
    p1 = matmul(h3, w1, BF16, 1088, 1024)
    lr = matmul(h3, wlr, F32, 1088, LANES)
    w2p = jnp.zeros((2, LANES, GLA_KW), F32)
    w2p = w2p.at[0, 0:GLA_RANK].set(l1_gate_w2[0]).at[1, GLA_RANK:2 * GLA_RANK].set(l1_gate_w2[1])
    gbp = l1_gate_b.reshape(2, 1, GLA_KW)
    gs_zero = jnp.zeros((B, GLA_H, GLA_DV, GLA_DK), F32)
    outs = []
    for d in range(2):
        _, s_c = gla_scan(p1, lr, w2p, gbp, gs_zero, d, NL, C, C)
        o_d, _ = gla_scan(p1, lr, w2p, gbp, s_c, d, 0, T, GLA_TB)
        outs.append(o_d)
    router_p = jnp.pad(l1_router, ((0, 0), (0, LANES - NE)))
    x3, h4u, route = l1_out(outs[0], outs[1], p1, x2, l1_w_out.astype(BF16), l1_o_norm.reshape(1, GLA_DV),
                            l1gt1, l1_norm2.reshape(1, D), l1sh2, l1sc2, router_p)

    pos, item_e, item_s, item_n, n_items = _route_tables(route)
    a_idx = jnp.arange(NASSIGN, dtype=I32)
    xs = row_move(a_idx % NL, pos, h4u, MOE_ROWS, jnp.zeros((MOE_ROWS, D // 2), U32))
    ys = moe_ffn(item_e, item_s, item_n, n_items, xs, l1_exp_wg, l1_exp_wu, l1_exp_wd)
    gth = row_move(pos, a_idx, ys, NASSIGN)
    out = moe_combine(x3, gth, route, l1gt2)
    return out.reshape(B, T, D)
```

```python
import functools

import jax
import jax.numpy as jnp
from jax import lax
from jax.experimental import pallas as pl
from jax.experimental.pallas import tpu as pltpu

F32 = jnp.float32
BF16 = jnp.bfloat16
I32 = jnp.int32
U32 = jnp.uint32

D = 2048
B = 2
T = 4096
C = 256
NL = B * T
NC = B * C
NR = NL + NC
GRID_W = 64
GRID_H = T // GRID_W
NORM_EPS = 1e-6

HD = 64
RW = 1024
RWKV_COLS = 3 * RW + 2 * 64 + 2 * 64 + 128
NA_W = 1024
NA_KH = 8
NA_KW = 16
RWKV_GN_EPS = 64e-5

GLA_H = 4
GLA_DK = 256
GLA_DV = 512
GLA_KW = 1024
GLA_VW = 2048
GLA_RANK = 16
GLA_NORMALIZER = 16.0
GLA_CHUNK = 64
GLA_SUB = 16

FFN = 5632
NE = 8
EDIM = 7168

LANES = 128
VMEM_BYTES = 64 * 1024 * 1024
VMEM_CAP = VMEM_BYTES - 8 * 1024 * 1024

TM = 512
SEG_TILES = T // TM
PREP_TM = 256
SCAN_TB = 64
NPAIR = RW // LANES
GLA_TB = 512
K10_TM = 256
MOE_SUB = 256
MOE_NSUB = 4
MOE_TF = 256
NASSIGN = 2 * NL
MOE_ROWS = NASSIGN + NE * MOE_SUB
MOE_ITEMS = MOE_ROWS // (MOE_SUB * MOE_NSUB) + NE


def _cparams(sem, block_bytes, scratch_bytes=0):
    est = 2 * block_bytes + scratch_bytes
    limit = min(VMEM_CAP, max(32 * 1024 * 1024, int(est * 1.3) + 12 * 1024 * 1024))
    return pltpu.CompilerParams(dimension_semantics=sem, vmem_limit_bytes=limit)


def _nbytes(shape, dtype):
    n = 1
    for s in shape:
        n *= s
    return n * jnp.dtype(dtype).itemsize


def _seg(i):
    return jnp.minimum(i // SEG_TILES, 2)


def _split2(x):
    hi = x.astype(BF16)
    lo = (x - hi.astype(F32)).astype(BF16)
    return hi, lo


def _split3(x):
    hi = x.astype(BF16)
    r = x - hi.astype(F32)
    mid = r.astype(BF16)
    lo = (r - mid.astype(F32)).astype(BF16)
    return hi, mid, lo


def _dot(a, b):
    return jnp.dot(a, b, preferred_element_type=F32)


def _dot_nt(a, b):
    return lax.dot_general(a, b, (((1,), (1,)), ((), ())), preferred_element_type=F32)


def _dot_tn(a, b):
    return lax.dot_general(a, b, (((0,), (0,)), ((), ())), preferred_element_type=F32)


def _dot3(a, b):
    ah, al = _split2(a)
    bh, bl = _split2(b)
    return _dot(ah, bh) + _dot(al, bh) + _dot(ah, bl)


def _dot_exact_lhs(m_bf16, x):
    h, m, l = _split3(x)
    return _dot(m_bf16, h) + _dot(m_bf16, m) + _dot(m_bf16, l)


def _group_sum(x, bo):
    outs = []
    for j in range(x.shape[-1] // LANES):
        h, l = _split2(x[:, j * LANES:(j + 1) * LANES])
        outs.append(_dot(h, bo) + _dot(l, bo))
    return outs[0] if len(outs) == 1 else jnp.concatenate(outs, axis=-1)


def _sigmoid(x):
    return 1.0 / (1.0 + jnp.exp(-x))


def _silu(x):
    return x * _sigmoid(x)


def _softplus(x):
    return jnp.maximum(x, 0.0) + jnp.log1p(jnp.exp(-jnp.abs(x)))


def _rms_mod(x, gain, shift, scale):
    ms = jnp.mean(x * x, axis=-1, keepdims=True)
    y = x * lax.rsqrt(ms + NORM_EPS) * gain
    return y * (1.0 + scale) + shift


def _adaln_kernel(c_ref, w_ref, b_ref, o_ref):
    a = _silu(c_ref[...])
    o_ref[...] = _dot3(a, w_ref[...]) + b_ref[...]


def adaln(cond8, w_mod, b_mod):
    n = w_mod.shape[1]
    tn = 1024
    return pl.pallas_call(
        _adaln_kernel,
        grid=(n // tn,),
        in_specs=[pl.BlockSpec((8, D), lambda j: (0, 0)),
                  pl.BlockSpec((D, tn), lambda j: (0, j)),
                  pl.BlockSpec((1, tn), lambda j: (0, j))],
        out_specs=pl.BlockSpec((8, tn), lambda j: (0, j)),
        out_shape=jax.ShapeDtypeStruct((8, n), F32),
        compiler_params=_cparams(("parallel",), _nbytes((D, tn), F32), 3 * _nbytes((D, tn), F32)),
    )(cond8, w_mod, b_mod.reshape(1, n))


def _mods(cond8, w_mod, b_mod):
    m = adaln(cond8, w_mod, b_mod)[:3].reshape(3, 6, 1, D)
    return [m[:, i] for i in range(6)]


def _normmod_kernel(x_ref, g_ref, sh_ref, sc_ref, o_ref):
    o_ref[...] = _rms_mod(x_ref[...], g_ref[...], sh_ref[...], sc_ref[...]).astype(BF16)


def norm_mod(x, gain, shift, scale):
    n = x.shape[0]
    vec = pl.BlockSpec((None, 1, D), lambda i: (_seg(i), 0, 0))
    return pl.pallas_call(
        _normmod_kernel,
        grid=(n // TM,),
        in_specs=[pl.BlockSpec((TM, D), lambda i: (i, 0)),
                  pl.BlockSpec((1, D), lambda i: (0, 0)), vec, vec],
        out_specs=pl.BlockSpec((TM, D), lambda i: (i, 0)),
        out_shape=jax.ShapeDtypeStruct((n, D), BF16),
        compiler_params=_cparams(("parallel",), _nbytes((TM, D), F32) * 2),
    )(x, gain.reshape(1, D), shift, scale)


def _mm_kernel(a_ref, w_ref, o_ref):
    o_ref[...] = _dot(a_ref[...], w_ref[...]).astype(o_ref.dtype)


def matmul(a, w, out_dtype, tm, tn):
    m, k = a.shape
    n = w.shape[1]
    blk = _nbytes((tm, k), a.dtype) + _nbytes((k, tn), w.dtype) + _nbytes((tm, tn), out_dtype)
    return pl.pallas_call(
        _mm_kernel,
        grid=(m // tm, n // tn),
        in_specs=[pl.BlockSpec((tm, k), lambda i, j: (i, 0)),
                  pl.BlockSpec((k, tn), lambda i, j: (0, j))],
        out_specs=pl.BlockSpec((tm, tn), lambda i, j: (i, j)),
        out_shape=jax.ShapeDtypeStruct((m, n), out_dtype),
        compiler_params=_cparams(("parallel", "parallel"), blk, _nbytes((tm, tn), F32)),
    )(a, w)


def _rwkv_prep_kernel(x_ref, xp_ref, xn_ref, mu_ref, wd_ref, wa_ref, wg_ref, w0_ref, a0_ref,
                      kk_ref, ka_ref, rk_ref, bo_ref,
                      r_out, v_out, nkk_out, g_out, bonus_out, w_out, kd_out, z_out):
    i = pl.program_id(0)
    lat_tiles = NL // PREP_TM
    seq_tiles = T // PREP_TM
    is_ctx = i >= lat_tiles
    first = jnp.logical_or(is_ctx, i % seq_tiles == 0)
    last = jnp.logical_or(is_ctx, i % seq_tiles == seq_tiles - 1)
    x = x_ref[...]
    prev_row = jnp.where(first, 0.0, xp_ref[7:8, :])
    next_row = jnp.where(last, 0.0, xn_ref[0:1, :])
    rows = lax.broadcasted_iota(I32, x.shape, 0)
    xp = jnp.where(rows == 0, prev_row, pltpu.roll(x, 1, axis=0))
    xn = jnp.where(rows == PREP_TM - 1, next_row, pltpu.roll(x, PREP_TM - 1, axis=0))
    ps = x + mu_ref[...] * (0.5 * (xp + xn) - x)

    bo = bo_ref[...]
    r = ps[:, 0:RW]
    k = ps[:, RW:2 * RW]
    v = ps[:, 2 * RW:3 * RW]
    dec_in = ps[:, 3 * RW:3 * RW + 128]
    icl_in = ps[:, 3 * RW + 128:3 * RW + 256]
    gate_in = ps[:, 3 * RW + 256:3 * RW + 384]
    lora_dec = _dot(jnp.tanh(dec_in).astype(BF16), wd_ref[...])
    lora_icl = _dot(icl_in.astype(BF16), wa_ref[...])
    g = _dot(_sigmoid(gate_in).astype(BF16), wg_ref[...])

    kk = k * kk_ref[...]
    kkn = kk * lax.rsqrt(_group_sum(kk * kk, bo) + 1e-12)
    ksum = jnp.zeros_like(k)
    for d in range(2):
        sl = slice(d * RW, (d + 1) * RW)
        w_log = -_softplus(-(w0_ref[:, sl] + lora_dec[:, sl])) - 0.5
        decay = jnp.exp(-jnp.exp(w_log))
        a = _sigmoid(a0_ref[:, sl] + lora_icl[:, sl])
        k_d = k * (1.0 + (a - 1.0) * ka_ref[...])
        ksum = ksum + k_d
        w_out[d] = decay
        kd_out[d] = k_d
        z_out[d] = kkn * a
    r_out[...] = r
    v_out[...] = v
    nkk_out[...] = -kkn
    g_out[...] = g
    bonus_out[...] = _group_sum(r * (0.5 * ksum) * rk_ref[...], bo) * v


def rwkv_prep(pr, mu, wd_blk, wa_blk, wg, w0, a0, k_k, k_a, r_k, bo):
    n = pr.shape[0]
    nt = n // PREP_TM
    cols = pr.shape[1]
    row = lambda c: pl.BlockSpec((1, c), lambda i: (0, 0))
    full = lambda s: pl.BlockSpec(s, lambda i: (0, 0))
    o1 = pl.BlockSpec((PREP_TM, RW), lambda i: (i, 0))
    o2 = pl.BlockSpec((2, PREP_TM, RW), lambda i: (0, i, 0))
    s1 = jax.ShapeDtypeStruct((n, RW), F32)
    s2 = jax.ShapeDtypeStruct((2, n, RW), F32)
    blk = _nbytes((PREP_TM, cols), F32) + 11 * _nbytes((PREP_TM, RW), F32)
    return pl.pallas_call(
        _rwkv_prep_kernel,
        grid=(nt,),
        in_specs=[pl.BlockSpec((PREP_TM, cols), lambda i: (i, 0)),
                  pl.BlockSpec((8, cols), lambda i: (jnp.maximum(i * (PREP_TM // 8) - 1, 0), 0)),
                  pl.BlockSpec((8, cols), lambda i: (jnp.minimum((i + 1) * (PREP_TM // 8), n // 8 - 1), 0)),
                  row(cols), full((128, 2 * RW)), full((128, 2 * RW)), full((128, RW)),
                  row(2 * RW), row(2 * RW), row(RW), row(RW), row(RW), full((128, 128))],
        out_specs=[o1, o1, o1, o1, o1, o2, o2, o2],
        out_shape=[s1, s1, s1, s1, s1, s2, s2, s2],
        compiler_params=_cparams(("parallel",), blk, 12 * _nbytes((PREP_TM, cols), F32)),
    )(pr, pr, pr, mu, wd_blk, wa_blk, wg, w0, a0, k_k, k_a, r_k, bo)


NCHAIN = B * NPAIR


def _rwkv_scan_kernel(*refs):
    nin = 2 * B * 6
    vec_refs = refs[:nin]
    s0_ref, bo_ref, m1_ref, yf_ref, yb_ref, sfin_ref, s_scr, yt_scr = refs[nin:]
    j = pl.program_id(0)
    nb = pl.num_programs(0)

    @pl.when(j == 0)
    def _():
        s_scr[...] = s0_ref[...]

    yt_scr[...] = jnp.zeros_like(yt_scr)
    bo = bo_ref[...]
    m1 = m1_ref[...]
    lane_mod = lax.broadcasted_iota(I32, (HD, LANES), 1) % HD
    shape3 = (NCHAIN, HD, LANES)
    half = NCHAIN // 2

    def bsum(x):
        x2 = jnp.concatenate([x[:half], x[half:]], axis=-1).reshape(half * HD, 2 * LANES).astype(BF16)
        y2 = _dot(x2, bo).reshape(half, HD, 2 * LANES)
        return jnp.concatenate([y2[..., :LANES], y2[..., LANES:]], axis=0)

    sub8 = 8
    ngroups = SCAN_TB // sub8

    def group(gi, carry):
        for u in range(sub8):
            for d in range(2):
                if d == 0:
                    base, off = gi * sub8, u
                else:
                    base, off = (ngroups - 1 - gi) * sub8, sub8 - 1 - u
                base = pl.multiple_of(base, sub8)
                t = base + off

                def row(which):
                    parts = []
                    for b in range(B):
                        ref = vec_refs[(d * B + b) * 6 + which]
                        for p in range(NPAIR):
                            parts.append(ref[pl.ds(base, sub8), p * LANES:(p + 1) * LANES][off:off + 1]
                                         .reshape(1, 1, LANES))
                    return jnp.broadcast_to(jnp.concatenate(parts, axis=0), shape3)

                st = s_scr[d]
                sa = bsum(st * row(2))
                vcol = bsum(m1 * row(1))
                st = st * row(3) + sa * row(5) + vcol * row(4)
                s_scr[d] = st
                ycol = bsum(st * row(0))
                yt_scr[d] = jnp.where(lane_mod == t, ycol, yt_scr[d])
        return carry

    lax.fori_loop(0, ngroups, group, 0)

    lane = lax.broadcasted_iota(I32, (HD, LANES), 1)
    sub = lax.broadcasted_iota(I32, (HD, LANES), 0)
    pick = jnp.where(lane_mod == sub, 1.0, 0.0).astype(BF16)
    for d, y_ref in ((0, yf_ref), (1, yb_ref)):
        for b in range(B):
            for p in range(NPAIR):
                yt = yt_scr[d, b * NPAIR + p]
                z = jnp.concatenate([jnp.where(lane < HD, yt, 0.0), jnp.where(lane >= HD, yt, 0.0)], axis=0)
                zh, zl = _split2(z)
                y_ref[b, :, p * LANES:(p + 1) * LANES] = _dot_nt(pick, zh) + _dot_nt(pick, zl)

    @pl.when(j == nb - 1)
    def _():
        sfin_ref[...] = s_scr[...]


def rwkv_scan(r, v, nkk, w, kd, z, s0, bo2, m1, row0, seq):
    nb = seq // SCAN_TB
    base = row0 // SCAN_TB

    def blk_idx(d, b):
        if d == 0:
            return lambda j: base + b * nb + j
        return lambda j: base + b * nb + nb - 1 - j

    def shared(idx):
        return pl.BlockSpec((SCAN_TB, RW), lambda j: (idx(j), 0))

    def per_dir(d, idx):
        return pl.BlockSpec((None, SCAN_TB, RW), lambda j: (d, idx(j), 0))

    in_specs, args = [], []
    for d in range(2):
        for b in range(B):
            idx = blk_idx(d, b)
            in_specs += [shared(idx), shared(idx), shared(idx), per_dir(d, idx), per_dir(d, idx), per_dir(d, idx)]
            args += [r, v, nkk, w, kd, z]
    st_shape = (2, NCHAIN, HD, LANES)
    st_spec = pl.BlockSpec(st_shape, lambda j: (0, 0, 0, 0))
    in_specs += [st_spec, pl.BlockSpec((2 * LANES, 2 * LANES), lambda j: (0, 0)),
                 pl.BlockSpec((HD, LANES), lambda j: (0, 0))]
    blk = (len(args) + 2 * B) * _nbytes((SCAN_TB, RW), F32) + 2 * _nbytes(st_shape, F32)
    return pl.pallas_call(
        _rwkv_scan_kernel,
        grid=(nb,),
        in_specs=in_specs,
        out_specs=[pl.BlockSpec((B, SCAN_TB, RW), lambda j: (0, j, 0)),
                   pl.BlockSpec((B, SCAN_TB, RW), lambda j: (0, nb - 1 - j, 0)),
                   st_spec],
        out_shape=[jax.ShapeDtypeStruct((B, seq, RW), F32),
                   jax.ShapeDtypeStruct((B, seq, RW), F32),
                   jax.ShapeDtypeStruct(st_shape, F32)],
        scratch_shapes=[pltpu.VMEM(st_shape, F32), pltpu.VMEM(st_shape, F32)],
        compiler_params=_cparams(("arbitrary",), blk, 2 * _nbytes(st_shape, F32)),
    )(*args, s0, bo2, m1)


def _na_norm(x, gain, bo):
    xf = x.astype(F32)
    ms = _group_sum(xf * xf, bo) * (1.0 / HD)
    return xf * lax.rsqrt(ms + NORM_EPS) * gain


def _na_kernel(q_ref, k_ref, v_ref, qc_ref, kc_ref, vc_ref, bias_ref, qg_ref, kg_ref, bo_ref,
               o_ref, oc_ref, qn_scr, kn_scr):
    bo = bo_ref[...]
    scale = HD ** -0.5
    nchunk = 8
    rows = T // nchunk

    def norm_chunk(c, carry):
        sl = pl.ds(pl.multiple_of(c * rows, rows), rows)
        qn_scr[sl, :] = (_na_norm(q_ref[sl, :], qg_ref[...], bo) * scale).astype(BF16)
        kn_scr[sl, :] = _na_norm(k_ref[sl, :], kg_ref[...], bo).astype(BF16)
        return carry

    lax.fori_loop(0, nchunk, norm_chunk, 0)
    kc = _na_norm(kc_ref[...], kg_ref[...], bo).astype(BF16)
    qc = (_na_norm(qc_ref[...], qg_ref[...], bo) * scale).astype(BF16)
    vc = vc_ref[...]

    def two_heads(q2):
        lane = lax.broadcasted_iota(I32, q2.shape, 1)
        zero = jnp.zeros_like(q2)
        return jnp.concatenate([jnp.where(lane < HD, q2, zero), jnp.where(lane >= HD, q2, zero)], axis=0)

    def merge_heads(o, n):
        lane = lax.broadcasted_iota(I32, (n, LANES), 1)
        return jnp.where(lane < HD, o[0:n], o[n:2 * n])

    s = _dot_nt(two_heads(qc), kc)
    p = jnp.exp(s - jnp.max(s, axis=-1, keepdims=True))
    o = _dot(p.astype(BF16), vc) / jnp.sum(p, axis=-1, keepdims=True)
    oc_ref[...] = merge_heads(o, C).astype(BF16)

    band = NA_KH * GRID_W

    def row_step(r, carry):
        r0 = jnp.clip(r - NA_KH // 2, 0, GRID_H - NA_KH)
        dr = r - r0
        q2 = qn_scr[pl.ds(pl.multiple_of(r * GRID_W, GRID_W), GRID_W), :]
        ksl = pl.ds(pl.multiple_of(r0 * GRID_W, GRID_W), band)
        lhs = two_heads(q2)
        bias = jnp.concatenate([bias_ref[0, dr], bias_ref[1, dr]], axis=0)
        s_w = _dot_nt(lhs, kn_scr[ksl, :]) + bias
        s_c = _dot_nt(lhs, kc)
        m = jnp.maximum(jnp.max(s_w, axis=-1, keepdims=True), jnp.max(s_c, axis=-1, keepdims=True))
        p_w = jnp.exp(s_w - m)
        p_c = jnp.exp(s_c - m)
        l = jnp.sum(p_w, axis=-1, keepdims=True) + jnp.sum(p_c, axis=-1, keepdims=True)
        o = (_dot(p_w.astype(BF16), v_ref[ksl, :]) + _dot(p_c.astype(BF16), vc)) / l
        o_ref[pl.ds(pl.multiple_of(r * GRID_W, GRID_W), GRID_W), :] = merge_heads(o, GRID_W).astype(BF16)
        return carry

    lax.fori_loop(0, GRID_H, row_step, 0)


def na_attention(pna, bias_tab, qg, kg, bo):
    hp = NA_W // LANES
    ctx0 = NL // C

    def lat(col0):
        return pl.BlockSpec((T, LANES), lambda b, p: (b, col0 + p))

    def ctx(col0):
        return pl.BlockSpec((C, LANES), lambda b, p: (ctx0 + b, col0 + p))

    vec = pl.BlockSpec((1, LANES), lambda b, p: (0, 0))
    blk = 3 * _nbytes((T, LANES), BF16) + _nbytes((2, NA_KH, GRID_W, NA_KH * GRID_W), F32) + _nbytes((T, LANES), BF16)
    return pl.pallas_call(
        _na_kernel,
        grid=(B, hp),
        in_specs=[lat(0), lat(hp), lat(2 * hp), ctx(0), ctx(hp), ctx(2 * hp),
                  pl.BlockSpec((2, NA_KH, GRID_W, NA_KH * GRID_W), lambda b, p: (p, 0, 0, 0)),
                  vec, vec, pl.BlockSpec((LANES, LANES), lambda b, p: (0, 0))],
        out_specs=[pl.BlockSpec((T, LANES), lambda b, p: (b, p)),
                   pl.BlockSpec((C, LANES), lambda b, p: (b, p))],
        out_shape=[jax.ShapeDtypeStruct((NL, NA_W), BF16), jax.ShapeDtypeStruct((NC, NA_W), BF16)],
        scratch_shapes=[pltpu.VMEM((T, LANES), BF16), pltpu.VMEM((T, LANES), BF16)],
        compiler_params=_cparams(("parallel", "parallel"), blk, 2 * _nbytes((T, LANES), BF16)),
    )(pna, pna, pna, pna, pna, pna, bias_tab, qg, kg, bo)


def _na_bias_table(rpb):
    nh, nr, nc = rpb.shape
    w = GRID_W
    span = 2 * w - 1
    pad = w - NA_KW
    f = jnp.pad(rpb, ((0, 0), (0, 0), (pad, span - nc - pad)))
    flat = jnp.pad(jnp.tile(f, (1, 1, w)), ((0, 0), (0, 0), (0, w)))
    toep = jnp.flip(flat.reshape(nh, nr, w, 2 * w), axis=2)[..., :w]
    q = jnp.arange(w)[:, None]
    kc = jnp.arange(w)[None, :]
    cs = jnp.clip(q - NA_KW // 2, 0, w - NA_KW)
    toep = jnp.where((kc >= cs) & (kc < cs + NA_KW), toep, -1e30)
    tab = jnp.stack([toep[:, NA_KH - 1 - dr:2 * NA_KH - 1 - dr] for dr in range(NA_KH)], axis=1)
    return tab.transpose(0, 1, 3, 2, 4).reshape(nh, NA_KH, w, NA_KH * w)


def _l0_out_kernel(yf_ref, yb_ref, bonus_ref, g_ref, bna_ref, x_ref, wa_ref, wb_ref,
                   lng_ref, lnb_ref, gt_ref, n2_ref, sh_ref, sc_ref, bo_ref, xo_ref, ho_ref):
    bo = bo_ref[...]
    y = yf_ref[...] + yb_ref[...]
    mu = _group_sum(y, bo) * (1.0 / HD)
    dlt = y - mu
    var = _group_sum(dlt * dlt, bo) * (1.0 / HD)
    yn = dlt * lax.rsqrt(var + RWKV_GN_EPS) * lng_ref[...] + lnb_ref[...]
    a = ((yn + bonus_ref[...]) * g_ref[...]).astype(BF16)
    acc = _dot(a, wa_ref[...]) + _dot(bna_ref[...], wb_ref[...])
    xo = x_ref[...] + gt_ref[...] * acc
    xo_ref[...] = xo
    ho_ref[...] = _rms_mod(xo, n2_ref[...], sh_ref[...], sc_ref[...]).astype(BF16)


def l0_out(yf, yb, bonus, g, bna, x, wa, wb, lnx_g, lnx_b, gt, norm2, sh, sc, bo):
    n = x.shape[0]
    half = pl.BlockSpec((TM, RW), lambda i: (i, 0))
    fullrow = pl.BlockSpec((TM, D), lambda i: (i, 0))
    wspec = pl.BlockSpec((RW, D), lambda i: (0, 0))
    vec = pl.BlockSpec((None, 1, D), lambda i: (_seg(i), 0, 0))
    r1 = pl.BlockSpec((1, RW), lambda i: (0, 0))
    blk = (4 * _nbytes((TM, RW), F32) + _nbytes((TM, RW), BF16) + 2 * _nbytes((TM, D), F32)
           + _nbytes((TM, D), BF16) + 2 * _nbytes((RW, D), BF16))
    return pl.pallas_call(
        _l0_out_kernel,
        grid=(n // TM,),
        in_specs=[half, half, half, half, half, fullrow, wspec, wspec, r1, r1, vec,
                  pl.BlockSpec((1, D), lambda i: (0, 0)), vec, vec,
                  pl.BlockSpec((LANES, LANES), lambda i: (0, 0))],
        out_specs=[fullrow, fullrow],
        out_shape=[jax.ShapeDtypeStruct((n, D), F32), jax.ShapeDtypeStruct((n, D), BF16)],
        compiler_params=_cparams(("parallel",), blk, 4 * _nbytes((TM, D), F32)),
    )(yf, yb, bonus, g, bna, x, wa, wb, lnx_g, lnx_b, gt, norm2, sh, sc, bo)


FFN_TF = 512


def _ffn_kernel(h_ref, x_ref, wg_ref, wu_ref, wd_ref, gt_ref, n_ref, sh_ref, sc_ref, xo_ref, ho_ref, acc):
    f = pl.program_id(1)

    @pl.when(f == 0)
    def _():
        acc[...] = jnp.zeros_like(acc)

    h = h_ref[...]
    act = (_silu(_dot(h, wg_ref[...])) * _dot(h, wu_ref[...])).astype(BF16)
    acc[...] += _dot(act, wd_ref[...])

    @pl.when(f == pl.num_programs(1) - 1)
    def _():
        xo = x_ref[...] + gt_ref[...] * acc[...]
        xo_ref[...] = xo
        ho_ref[...] = _rms_mod(xo, n_ref[...], sh_ref[...], sc_ref[...]).astype(BF16)


def ffn(h, x, wg, wu, wd, gt, norm_next, sh_next, sc_next):
    n = x.shape[0]
    rowblk = pl.BlockSpec((TM, D), lambda i, f: (i, 0))
    vec = pl.BlockSpec((None, 1, D), lambda i, f: (_seg(i), 0, 0))
    blk = (_nbytes((TM, D), BF16) * 2 + _nbytes((TM, D), F32) * 2
           + 3 * _nbytes((D, FFN_TF), BF16))
    return pl.pallas_call(
        _ffn_kernel,
        grid=(n // TM, FFN // FFN_TF),
        in_specs=[rowblk, rowblk,
                  pl.BlockSpec((D, FFN_TF), lambda i, f: (0, f)),
                  pl.BlockSpec((D, FFN_TF), lambda i, f: (0, f)),
                  pl.BlockSpec((FFN_TF, D), lambda i, f: (f, 0)),
                  vec, pl.BlockSpec((1, D), lambda i, f: (0, 0)), vec, vec],
        out_specs=[rowblk, rowblk],
        out_shape=[jax.ShapeDtypeStruct((n, D), F32), jax.ShapeDtypeStruct((n, D), BF16)],
        scratch_shapes=[pltpu.VMEM((TM, D), F32)],
        compiler_params=_cparams(("parallel", "arbitrary"), blk, 3 * _nbytes((TM, D), F32)),
    )(h, x, wg, wu, wd, gt, norm_next, sh_next, sc_next)


def _gla_kernel(k_ref, v_ref, q_ref, lr_ref, w2_ref, gb_ref, s0_ref, o_ref, sfin_ref, s_scr, *, rev, nchunks):
    j = pl.program_id(2)

    @pl.when(j == 0)
    def _():
        s_scr[...] = s0_ref[...]

    L = GLA_CHUNK
    row = lax.broadcasted_iota(I32, (L, L), 0)
    col = lax.broadcasted_iota(I32, (L, L), 1)
    tri = jnp.where((col >= row) if rev else (col <= row), 1.0, 0.0).astype(BF16)
    nsb = L // GLA_SUB

    def chunk(ci, carry):
        c = nchunks - 1 - ci if rev else ci
        sl = pl.ds(pl.multiple_of(c * L, L), L)
        k = k_ref[sl, :].astype(F32)
        v = v_ref[sl, :]
        q = q_ref[sl, :].astype(F32) * (GLA_DK ** -0.5)
        pre = _dot3(lr_ref[sl, :], w2_ref[...]) + gb_ref[...]
        g = (jnp.minimum(pre, 0.0) - jnp.log1p(jnp.exp(-jnp.abs(pre)))) * (1.0 / GLA_NORMALIZER)
        b = _dot_exact_lhs(tri, g)
        bend = b[0:1] if rev else b[L - 1:L]
        st = s_scr[...]
        o_inter = _dot_nt((q * jnp.exp(b)).astype(BF16), st.astype(BF16))
        blocks = []
        for i in range(nsb):
            if rev:
                r_lo, r_hi = L - GLA_SUB * (i + 1), L - GLA_SUB * i
                k_lo, k_hi = r_lo, L
                ref = b[r_hi:r_hi + 1] if i > 0 else None
            else:
                r_lo, r_hi = GLA_SUB * i, GLA_SUB * (i + 1)
                k_lo, k_hi = 0, r_hi
                ref = b[r_lo - 1:r_lo] if i > 0 else None
            bq = b[r_lo:r_hi]
            bk = b[k_lo:k_hi]
            if ref is None:
                qe = q[r_lo:r_hi] * jnp.exp(bq)
                ke = k[k_lo:k_hi] * jnp.exp(-bk)
            else:
                qe = q[r_lo:r_hi] * jnp.exp(bq - ref)
                ke = k[k_lo:k_hi] * jnp.exp(ref - bk)
            att = _dot_nt(qe.astype(BF16), ke.astype(BF16))
            nk = k_hi - k_lo
            tq = lax.broadcasted_iota(I32, (GLA_SUB, nk), 0)
            sk = lax.broadcasted_iota(I32, (GLA_SUB, nk), 1)
            ok = (sk >= tq) if rev else (sk <= tq + r_lo)
            att = jnp.where(ok, att, 0.0)
            blocks.append((r_lo, _dot(att.astype(BF16), v[k_lo:k_hi])))
        blocks.sort(key=lambda t: t[0])
        o_ref[sl, :] = o_inter + jnp.concatenate([blk for _, blk in blocks], axis=0)
        kd = (k * jnp.exp(bend - b)).astype(BF16)
        s_scr[...] = st * jnp.exp(bend) + _dot_tn(v, kd)
        return carry

    lax.fori_loop(0, nchunks, chunk, 0)

    @pl.when(j == pl.num_programs(2) - 1)
    def _():
        sfin_ref[...] = s_scr[...]


def gla_scan(p1, lr, w2p, gbp, s0, d, row0, seq, tb):
    rev = d == 1
    nb = seq // tb
    base = row0 // tb
    nchunks = tb // GLA_CHUNK

    def blkidx(b, j):
        return base + b * nb + (nb - 1 - j if rev else j)

    kcol = GLA_KW // GLA_DK
    st_spec = pl.BlockSpec((None, None, GLA_DV, GLA_DK), lambda b, h, j: (b, h, 0, 0))
    blk = (2 * _nbytes((tb, GLA_DK), BF16) + _nbytes((tb, GLA_DV), BF16) + _nbytes((tb, LANES), F32)
           + _nbytes((tb, GLA_DV), F32) + 2 * _nbytes((GLA_DV, GLA_DK), F32))
    return pl.pallas_call(
        functools.partial(_gla_kernel, rev=rev, nchunks=nchunks),
        grid=(B, GLA_H, nb),
        in_specs=[pl.BlockSpec((tb, GLA_DK), lambda b, h, j: (blkidx(b, j), h)),
                  pl.BlockSpec((tb, GLA_DV), lambda b, h, j: (blkidx(b, j), GLA_KW // GLA_DV + h)),
                  pl.BlockSpec((tb, GLA_DK), lambda b, h, j: (blkidx(b, j), (GLA_KW + GLA_VW) // GLA_DK + h)),
                  pl.BlockSpec((tb, LANES), lambda b, h, j: (blkidx(b, j), 0)),
                  pl.BlockSpec((None, LANES, GLA_DK), lambda b, h, j: (d, 0, h)),
                  pl.BlockSpec((None, 1, GLA_DK), lambda b, h, j: (d, 0, h)),
                  st_spec],
        out_specs=[pl.BlockSpec((tb, GLA_DV), lambda b, h, j: (b * nb + (nb - 1 - j if rev else j), h)),
                   st_spec],
        out_shape=[jax.ShapeDtypeStruct((B * seq, GLA_VW), F32),
                   jax.ShapeDtypeStruct((B, GLA_H, GLA_DV, GLA_DK), F32)],
        scratch_shapes=[pltpu.VMEM((GLA_DV, GLA_DK), F32)],
        compiler_params=_cparams(("parallel", "parallel", "arbitrary"), blk, 8 * _nbytes((GLA_DV, GLA_DK), F32)),
    )(p1, p1, p1, lr, w2p, gbp, s0)


def _l1_out_kernel(of_ref, ob_ref, gate_ref, x_ref, w_ref, on_ref, gt_ref, n2_ref, sh_ref, sc_ref, rt_ref,
                   xo_ref, hu_ref, route_ref):
    o = of_ref[...] + ob_ref[...]
    heads = []
    for h in range(GLA_H):
        oh = o[:, h * GLA_DV:(h + 1) * GLA_DV]
        ms = jnp.mean(oh * oh, axis=-1, keepdims=True)
        heads.append(oh * lax.rsqrt(ms + NORM_EPS) * on_ref[...])
    og = (jnp.concatenate(heads, axis=-1) * _silu(gate_ref[...].astype(F32))).astype(BF16)
    xo = x_ref[...] + gt_ref[...] * _dot(og, w_ref[...])
    xo_ref[...] = xo
    h4 = _rms_mod(xo, n2_ref[...], sh_ref[...], sc_ref[...])

    logits = _dot3(h4, rt_ref[...])
    lane = lax.broadcasted_iota(I32, logits.shape, 1)
    neg = -jnp.inf
    lg = jnp.where(lane < NE, logits, neg)
    m1 = jnp.max(lg, axis=-1, keepdims=True)
    i1 = jnp.min(jnp.where(lg == m1, lane, LANES), axis=-1, keepdims=True)
    lg2 = jnp.where(lane == i1, neg, lg)
    m2 = jnp.max(lg2, axis=-1, keepdims=True)
    i2 = jnp.min(jnp.where(lg2 == m2, lane, LANES), axis=-1, keepdims=True)
    e2 = jnp.exp(m2 - m1)
    w1 = 1.0 / (1.0 + e2)
    w2 = e2 / (1.0 + e2)
    route = jnp.where(lane == 0, i1.astype(F32),
                      jnp.where(lane == 1, i2.astype(F32),
                                jnp.where(lane == 2, w1, jnp.where(lane == 3, w2, 0.0))))
    route_ref[...] = route

    hb = lax.bitcast_convert_type(h4.astype(BF16).astype(F32), U32)
    hu_ref[...] = (hb[:, :D // 2] >> 16) | (hb[:, D // 2:] & jnp.uint32(0xFFFF0000))


def l1_out(of, ob, p1, x, w_out, o_norm, gt, norm2, sh, sc, router_p):
    n = NL
    tm = K10_TM
    seg = lambda i: i // (T // tm)
    rowblk = pl.BlockSpec((tm, D), lambda i: (i, 0))
    vec = pl.BlockSpec((None, 1, D), lambda i: (seg(i), 0, 0))
    blk = (3 * _nbytes((tm, D), F32) + _nbytes((tm, D), BF16) + _nbytes((D, D), BF16)
           + _nbytes((D, LANES), F32) + _nbytes((tm, D), F32) + _nbytes((tm, D // 2), U32))
    return pl.pallas_call(
        _l1_out_kernel,
        grid=(n // tm,),
        in_specs=[rowblk, rowblk,
                  pl.BlockSpec((tm, GLA_VW), lambda i: (i, (2 * GLA_KW + GLA_VW) // GLA_VW)),
                  rowblk, pl.BlockSpec((D, D), lambda i: (0, 0)),
                  pl.BlockSpec((1, GLA_DV), lambda i: (0, 0)), vec,
                  pl.BlockSpec((1, D), lambda i: (0, 0)), vec, vec,
                  pl.BlockSpec((D, LANES), lambda i: (0, 0))],
        out_specs=[rowblk, pl.BlockSpec((tm, D // 2), lambda i: (i, 0)),
                   pl.BlockSpec((tm, LANES), lambda i: (i, 0))],
        out_shape=[jax.ShapeDtypeStruct((n, D), F32), jax.ShapeDtypeStruct((n, D // 2), U32),
                   jax.ShapeDtypeStruct((n, LANES), F32)],
        compiler_params=_cparams(("parallel",), blk, 4 * _nbytes((tm, D), F32)),
    )(of, ob, p1, x, w_out, o_norm, gt, norm2, sh, sc, router_p)


def _dispatch_kernel(pos_ref, h_ref, init_ref, xs_ref, sem):
    del init_ref
    base = pl.program_id(0) * TM

    def copy(r, slot):
        dst = pos_ref[slot * NL + base + r]
        return pltpu.make_async_copy(h_ref.at[pl.ds(r, 1)], xs_ref.at[pl.ds(dst, 1)], sem)

    def issue(r, carry):
        copy(r, 0).start()
        copy(r, 1).start()
        return carry

    def drain(r, carry):
        copy(r, 0).wait()
        copy(r, 1).wait()
        return carry

    lax.fori_loop(0, TM, issue, 0)
    lax.fori_loop(0, TM, drain, 0)


def moe_dispatch(pos, h4u, xs_init):
    return pl.pallas_call(
        _dispatch_kernel,
        grid_spec=pltpu.PrefetchScalarGridSpec(
            num_scalar_prefetch=1,
            grid=(NL // TM,),
            in_specs=[pl.BlockSpec((TM, D // 2), lambda i, pos: (i, 0)),
                      pl.BlockSpec(memory_space=pl.ANY)],
            out_specs=pl.BlockSpec(memory_space=pl.ANY),
            scratch_shapes=[pltpu.SemaphoreType.DMA(())]),
        out_shape=jax.ShapeDtypeStruct((MOE_ROWS, D // 2), U32),
        input_output_aliases={2: 0},
        compiler_params=pltpu.CompilerParams(dimension_semantics=("arbitrary",)),
    )(pos, h4u, xs_init)


def _moe_kernel(ie_ref, is_ref, in_ref, nit_ref, *refs):
    x_refs = refs[0:MOE_NSUB]
    wg_ref, wu_ref, wd_ref, ys_ref, xb, acc, sem = refs[MOE_NSUB:]
    i = pl.program_id(0)
    f = pl.program_id(1)
    valid = i < nit_ref[0]

    @pl.when(jnp.logical_and(valid, f == 0))
    def _():
        for s in range(MOE_NSUB):
            w = x_refs[s][...]
            lo = lax.bitcast_convert_type(w << 16, F32)
            hi = lax.bitcast_convert_type(w & jnp.uint32(0xFFFF0000), F32)
            xb[s * MOE_SUB:(s + 1) * MOE_SUB, :] = jnp.concatenate([lo, hi], axis=-1).astype(BF16)
        acc[...] = jnp.zeros_like(acc)

    @pl.when(valid)
    def _():
        x = xb[...]
        act = (_silu(_dot(x, wg_ref[...].astype(BF16))) * _dot(x, wu_ref[...].astype(BF16))).astype(BF16)
        acc[...] += _dot(act, wd_ref[...].astype(BF16))

    @pl.when(jnp.logical_and(valid, f == pl.num_programs(1) - 1))
    def _():
        start = is_ref[i]
        nsub = in_ref[i]
        for s in range(MOE_NSUB):
            @pl.when(s < nsub)
            def _():
                dst = ys_ref.at[pl.ds(pl.multiple_of((start + s) * MOE_SUB, MOE_SUB), MOE_SUB)]
                cp = pltpu.make_async_copy(acc.at[pl.ds(s * MOE_SUB, MOE_SUB)], dst, sem)
                cp.start()
                cp.wait()

    @pl.when(jnp.logical_and(i == pl.num_programs(0) - 1, f == pl.num_programs(1) - 1))
    def _():
        used = nit_ref[1]
        acc[0:MOE_SUB, :] = jnp.zeros((MOE_SUB, D), F32)
        for u in range(NE):
            @pl.when(used + u < MOE_ROWS // MOE_SUB)
            def _():
                dst = ys_ref.at[pl.ds(pl.multiple_of((used + u) * MOE_SUB, MOE_SUB), MOE_SUB)]
                cp = pltpu.make_async_copy(acc.at[pl.ds(0, MOE_SUB)], dst, sem)
                cp.start()
                cp.wait()


def moe_ffn(item_e, item_s, item_n, n_items, xs, wg, wu, wd):
    nf = EDIM // MOE_TF
    rows = MOE_SUB * MOE_NSUB

    def item(i, nit):
        return jnp.minimum(i, nit[0] - 1)

    def sub_idx(s):
        def idx(i, f, ie, ist, inn, nit):
            ii = item(i, nit)
            return (ist[ii] + jnp.minimum(s, inn[ii] - 1), 0)
        return idx

    x_specs = [pl.BlockSpec((MOE_SUB, D // 2), sub_idx(s)) for s in range(MOE_NSUB)]
    w_in = lambda i, f, ie, ist, inn, nit: (ie[item(i, nit)], 0, f)
    w_dn = lambda i, f, ie, ist, inn, nit: (ie[item(i, nit)], f, 0)
    blk = MOE_NSUB * _nbytes((MOE_SUB, D // 2), U32) + 3 * _nbytes((D, MOE_TF), F32)
    scratch = (_nbytes((rows, D), BF16) + _nbytes((rows, D), F32) + 6 * _nbytes((D, MOE_TF), BF16)
               + 3 * _nbytes((rows, MOE_TF), F32))
    return pl.pallas_call(
        _moe_kernel,
        grid_spec=pltpu.PrefetchScalarGridSpec(
            num_scalar_prefetch=4,
            grid=(MOE_ITEMS, nf),
            in_specs=x_specs + [pl.BlockSpec((None, D, MOE_TF), w_in),
                                pl.BlockSpec((None, D, MOE_TF), w_in),
                                pl.BlockSpec((None, MOE_TF, D), w_dn)],
            out_specs=pl.BlockSpec(memory_space=pl.ANY),
            scratch_shapes=[pltpu.VMEM((rows, D), BF16), pltpu.VMEM((rows, D), F32),
                            pltpu.SemaphoreType.DMA(())]),
        out_shape=jax.ShapeDtypeStruct((MOE_ROWS, D), F32),
        compiler_params=_cparams(("arbitrary", "arbitrary"), blk, scratch),
    )(item_e, item_s, item_n, n_items, *([xs] * MOE_NSUB), wg, wu, wd)


COMB_TM = 256


def _combine_kernel(pos_ref, x_ref, route_ref, gt_ref, ys_ref, o_ref, g0, g1, sem):
    base = pl.program_id(0) * COMB_TM

    def copy(r, slot, buf):
        src = pos_ref[slot * NL + base + r]
        return pltpu.make_async_copy(ys_ref.at[pl.ds(src, 1)], buf.at[pl.ds(r, 1)], sem)

    def issue(r, carry):
        copy(r, 0, g0).start()
        copy(r, 1, g1).start()
        return carry

    def drain(r, carry):
        copy(r, 0, g0).wait()
        copy(r, 1, g1).wait()
        return carry

    lax.fori_loop(0, COMB_TM, issue, 0)
    lax.fori_loop(0, COMB_TM, drain, 0)
    rt = route_ref[...]
    y = rt[:, 2:3] * g0[...] + rt[:, 3:4] * g1[...]
    o_ref[...] = x_ref[...] + gt_ref[...] * y


def moe_combine(pos, x, route, gt, ys):
    rowblk = pl.BlockSpec((COMB_TM, D), lambda i, pos: (i, 0))
    return pl.pallas_call(
        _combine_kernel,
        grid_spec=pltpu.PrefetchScalarGridSpec(
            num_scalar_prefetch=1,
            grid=(NL // COMB_TM,),
            in_specs=[rowblk,
                      pl.BlockSpec((COMB_TM, LANES), lambda i, pos: (i, 0)),
                      pl.BlockSpec((None, 1, D), lambda i, pos: (i // (T // COMB_TM), 0, 0)),
                      pl.BlockSpec(memory_space=pl.ANY)],
            out_specs=rowblk,
            scratch_shapes=[pltpu.VMEM((COMB_TM, D), F32), pltpu.VMEM((COMB_TM, D), F32),
                            pltpu.SemaphoreType.DMA(())]),
        out_shape=jax.ShapeDtypeStruct((NL, D), F32),
        compiler_params=_cparams(("arbitrary",), 2 * _nbytes((COMB_TM, D), F32), 2 * _nbytes((COMB_TM, D), F32)),
    )(pos, x, route, gt, ys)


def _block_ones(n=LANES):
    i = jnp.arange(n)
    return (i[:, None] // HD == i[None, :] // HD).astype(BF16)


def _diag_pick():
    return (jnp.arange(LANES)[None, :] % HD == jnp.arange(HD)[:, None]).astype(F32)


def _blockdiag2(w):
    z = jnp.zeros_like(w[0])
    return jnp.concatenate([jnp.concatenate([w[0], z], axis=1), jnp.concatenate([z, w[1]], axis=1)], axis=0)


def _route_tables(route):
    e = jnp.concatenate([route[:, 0], route[:, 1]]).astype(I32)
    onehot = (e[:, None] == jnp.arange(NE, dtype=I32)[None, :]).astype(I32)
    cum = jnp.cumsum(onehot, axis=0)
    rank = jnp.sum(onehot * cum, axis=1) - 1
    counts = cum[-1]
    nsub = (counts + MOE_SUB - 1) // MOE_SUB
    sub_start = jnp.cumsum(nsub) - nsub
    pos = (sub_start * MOE_SUB)[e] + rank
    nitem_e = (nsub + MOE_NSUB - 1) // MOE_NSUB
    item_end = jnp.cumsum(nitem_e)
    item_start = item_end - nitem_e
    ii = jnp.arange(MOE_ITEMS, dtype=I32)
    item_e = jnp.minimum(jnp.sum((ii[:, None] >= item_end[None, :]).astype(I32), axis=1), NE - 1)
    local = ii - item_start[item_e]
    item_s = sub_start[item_e] + MOE_NSUB * local
    item_n = jnp.clip(nsub[item_e] - MOE_NSUB * local, 1, MOE_NSUB)
    n_items = jnp.stack([item_end[-1], jnp.sum(nsub)]).astype(I32)
    return pos.astype(I32), item_e.astype(I32), item_s.astype(I32), item_n.astype(I32), n_items


def kernel(x, c, ctx, c_ctx, l0_w_mod, l0_b_mod, l0_norm1, l0_norm2, l0_w_in, l0_mu_shift, l0_decay_w0, l0_decay_w2, l0_iclr_a0, l0_iclr_a2, l0_gate_g2, l0_k_k, l0_k_a, l0_r_k, l0_lnx_g, l0_lnx_b, l0_q_norm, l0_k_norm, l0_rpb, l0_w_out, l0_ffn_wg, l0_ffn_wu, l0_ffn_wd, l1_w_mod, l1_b_mod, l1_norm1, l1_norm2, l1_w_in, l1_gate_w2, l1_gate_b, l1_o_norm, l1_w_out, l1_router, l1_exp_wg, l1_exp_wu, l1_exp_wd):
    bo = _block_ones()
    m1 = _diag_pick()
    xr = jnp.concatenate([x.reshape(NL, D), ctx.reshape(NC, D)], axis=0)
    cond8 = jnp.zeros((8, D), F32).at[0:B].set(c).at[B].set(c_ctx)

    sh1, sc1, gt1, sh2, sc2, gt2 = _mods(cond8, l0_w_mod, l0_b_mod)
    l1sh1, l1sc1, l1gt1, l1sh2, l1sc2, l1gt2 = _mods(cond8, l1_w_mod, l1_b_mod)
    h = norm_mod(xr, l0_norm1, sh1, sc1)
    pr = matmul(h, l0_w_in[:, :RWKV_COLS].astype(BF16), F32, 1088, 1152)
    pna = matmul(h, l0_w_in[:, RWKV_COLS:].astype(BF16), BF16, 1088, 1024)

    r, v, nkk, g, bonus, w, kd, z = rwkv_prep(
        pr, l0_mu_shift.reshape(1, -1),
        _blockdiag2(l0_decay_w2).astype(BF16), _blockdiag2(l0_iclr_a2).astype(BF16), l0_gate_g2.astype(BF16),
        l0_decay_w0.reshape(1, -1), l0_iclr_a0.reshape(1, -1),
        l0_k_k.reshape(1, -1), l0_k_a.reshape(1, -1), l0_r_k.reshape(1, -1), bo)
    bo2 = _block_ones(2 * LANES)
    s_zero = jnp.zeros((2, NCHAIN, HD, LANES), F32)
    ycf, ycb, s_ctx = rwkv_scan(r, v, nkk, w, kd, z, s_zero, bo2, m1, NL, C)
    ylf, ylb, _ = rwkv_scan(r, v, nkk, w, kd, z, s_ctx, bo2, m1, 0, T)
    yf = jnp.concatenate([ylf.reshape(NL, RW), ycf.reshape(NC, RW)], axis=0)
    yb = jnp.concatenate([ylb.reshape(NL, RW), ycb.reshape(NC, RW)], axis=0)

    b_l, b_c = na_attention(pna, _na_bias_table(l0_rpb),
                            jnp.tile(l0_q_norm, 2).reshape(1, LANES), jnp.tile(l0_k_norm, 2).reshape(1, LANES), bo)
    bna = jnp.concatenate([b_l, b_c], axis=0)

    w_out0 = l0_w_out.astype(BF16)
    x1, h2 = l0_out(yf, yb, bonus, g, bna, xr, w_out0[:RW], w_out0[RW:],
                    l0_lnx_g.reshape(1, RW), l0_lnx_b.reshape(1, RW), gt1, l0_norm2.reshape(1, D), sh2, sc2, bo)
    x2, h3 = ffn(h2, x1, l0_ffn_wg.astype(BF16), l0_ffn_wu.astype(BF16), l0_ffn_wd.astype(BF16),
                 gt2, l1_norm1.reshape(1, D), l1sh1, l1sc1)

    st = GLA_KW + GLA_VW
    w1 = jnp.concatenate([l1_w_in[:, :st], l1_w_in[:, st + 2 * GLA_RANK:]], axis=1).astype(BF16)
    wlr = jnp.pad(l1_w_in[:, st:st + 2 * GLA_RANK], ((0, 0), (0, LANES - 2 * GLA_RANK))).astype(BF16)
    p1 = matmul(h3, w1, BF16, 1088, 1024)
    lr = matmul(h3, wlr, F32, 1088, LANES)
    w2p = jnp.zeros((2, LANES, GLA_KW), F32)
    w2p = w2p.at[0, 0:GLA_RANK].set(l1_gate_w2[0]).at[1, GLA_RANK:2 * GLA_RANK].set(l1_gate_w2[1])
    gbp = l1_gate_b.reshape(2, 1, GLA_KW)
    gs_zero = jnp.zeros((B, GLA_H, GLA_DV, GLA_DK), F32)
    outs = []
    for d in range(2):
        _, s_c = gla_scan(p1, lr, w2p, gbp, gs_zero, d, NL, C, C)
        o_d, _ = gla_scan(p1, lr, w2p, gbp, s_c, d, 0, T, GLA_TB)
        outs.append(o_d)
    router_p = jnp.pad(l1_router, ((0, 0), (0, LANES - NE)))
    x3, h4u, route = l1_out(outs[0], outs[1], p1, x2, l1_w_out.astype(BF16), l1_o_norm.reshape(1, GLA_DV),
                            l1gt1, l1_norm2.reshape(1, D), l1sh2, l1sc2, router_p)

    pos, item_e, item_s, item_n, n_items = _route_tables(route)
    xs = moe_dispatch(pos, h4u, jnp.zeros((MOE_ROWS, D // 2), U32))
    ys = moe_ffn(item_e, item_s, item_n, n_items, xs, l1_exp_wg, l1_exp_wu, l1_exp_wd)
    out = moe_combine(pos, x3, route, l1gt2, ys)
    return out.reshape(B, T, D)
```

```python
import functools

import jax
import jax.numpy as jnp
from jax import lax
from jax.experimental import pallas as pl
from jax.experimental.pallas import tpu as pltpu

F32 = jnp.float32
BF16 = jnp.bfloat16
I32 = jnp.int32

D = 2048
B = 2
T = 4096
C = 256
NL = B * T
NC = B * C
NR = NL + NC
GRID_W = 64
GRID_H = T // GRID_W
NORM_EPS = 1e-6

HD = 64
RW = 1024
RWKV_COLS = 3 * RW + 2 * 64 + 2 * 64 + 128
NA_W = 1024
NA_KH = 8
NA_KW = 16
RWKV_GN_EPS = 64e-5

GLA_H = 4
GLA_DK = 256
GLA_DV = 512
GLA_KW = 1024
GLA_VW = 2048
GLA_RANK = 16
GLA_NORMALIZER = 16.0
GLA_CHUNK = 64
GLA_SUB = 16

FFN = 5632
NE = 8
EDIM = 7168

LANES = 128
VMEM_BYTES = 64 * 1024 * 1024
VMEM_CAP = VMEM_BYTES - 8 * 1024 * 1024

TM = 512
SEG_TILES = T // TM
PREP_TM = 256
SCAN_TB = 64
NPAIR = RW // LANES
GLA_TB = 512
K10_TM = 256
MOE_SUB = 256
MOE_NSUB = 4
MOE_TF = 256
NASSIGN = 2 * NL
MOE_ROWS = NASSIGN + NE * MOE_SUB
MOE_ITEMS = MOE_ROWS // (MOE_SUB * MOE_NSUB) + NE


def _cparams(sem, block_bytes, scratch_bytes=0):
    est = 2 * block_bytes + scratch_bytes
    limit = min(VMEM_CAP, max(32 * 1024 * 1024, int(est * 1.3) + 12 * 1024 * 1024))
    return pltpu.CompilerParams(dimension_semantics=sem, vmem_limit_bytes=limit)


def _nbytes(shape, dtype):
    n = 1
    for s in shape:
        n *= s
    return n * jnp.dtype(dtype).itemsize


def _seg(i):
    return jnp.minimum(i // SEG_TILES, 2)


def _split2(x):
    hi = x.astype(BF16)
    lo = (x - hi.astype(F32)).astype(BF16)
    return hi, lo


def _split3(x):
    hi = x.astype(BF16)
    r = x - hi.astype(F32)
    mid = r.astype(BF16)
    lo = (r - mid.astype(F32)).astype(BF16)
    return hi, mid, lo


def _dot(a, b):
    return jnp.dot(a, b, preferred_element_type=F32)


def _dot_nt(a, b):
    return lax.dot_general(a, b, (((1,), (1,)), ((), ())), preferred_element_type=F32)


def _dot_tn(a, b):
    return lax.dot_general(a, b, (((0,), (0,)), ((), ())), preferred_element_type=F32)


def _dot3(a, b):
    ah, al = _split2(a)
    bh, bl = _split2(b)
    return _dot(ah, bh) + _dot(al, bh) + _dot(ah, bl)


def _dot_exact_lhs(m_bf16, x):
    h, m, l = _split3(x)
    return _dot(m_bf16, h) + _dot(m_bf16, m) + _dot(m_bf16, l)


def _group_sum(x, bo):
    outs = []
    for j in range(x.shape[-1] // LANES):
        h, l = _split2(x[:, j * LANES:(j + 1) * LANES])
        outs.append(_dot(h, bo) + _dot(l, bo))
    return outs[0] if len(outs) == 1 else jnp.concatenate(outs, axis=-1)


def _sigmoid(x):
    return 1.0 / (1.0 + jnp.exp(-x))


def _silu(x):
    return x * _sigmoid(x)


def _softplus(x):
    return jnp.maximum(x, 0.0) + jnp.log1p(jnp.exp(-jnp.abs(x)))


def _rms_mod(x, gain, shift, scale):
    ms = jnp.mean(x * x, axis=-1, keepdims=True)
    y = x * lax.rsqrt(ms + NORM_EPS) * gain
    return y * (1.0 + scale) + shift


def _adaln_kernel(c_ref, w_ref, b_ref, o_ref):
    a = _silu(c_ref[...])
    o_ref[...] = _dot3(a, w_ref[...]) + b_ref[...]


def adaln(cond8, w_mod, b_mod):
    n = w_mod.shape[1]
    tn = 1024
    return pl.pallas_call(
        _adaln_kernel,
        grid=(n // tn,),
        in_specs=[pl.BlockSpec((8, D), lambda j: (0, 0)),
                  pl.BlockSpec((D, tn), lambda j: (0, j)),
                  pl.BlockSpec((1, tn), lambda j: (0, j))],
        out_specs=pl.BlockSpec((8, tn), lambda j: (0, j)),
        out_shape=jax.ShapeDtypeStruct((8, n), F32),
        compiler_params=_cparams(("parallel",), _nbytes((D, tn), F32), 3 * _nbytes((D, tn), F32)),
    )(cond8, w_mod, b_mod.reshape(1, n))


def _mods(cond8, w_mod, b_mod):
    m = adaln(cond8, w_mod, b_mod)[:3].reshape(3, 6, 1, D)
    return [m[:, i] for i in range(6)]


def _normmod_kernel(x_ref, g_ref, sh_ref, sc_ref, o_ref):
    o_ref[...] = _rms_mod(x_ref[...], g_ref[...], sh_ref[...], sc_ref[...]).astype(BF16)


def norm_mod(x, gain, shift, scale):
    n = x.shape[0]
    vec = pl.BlockSpec((None, 1, D), lambda i: (_seg(i), 0, 0))
    return pl.pallas_call(
        _normmod_kernel,
        grid=(n // TM,),
        in_specs=[pl.BlockSpec((TM, D), lambda i: (i, 0)),
                  pl.BlockSpec((1, D), lambda i: (0, 0)), vec, vec],
        out_specs=pl.BlockSpec((TM, D), lambda i: (i, 0)),
        out_shape=jax.ShapeDtypeStruct((n, D), BF16),
        compiler_params=_cparams(("parallel",), _nbytes((TM, D), F32) * 2),
    )(x, gain.reshape(1, D), shift, scale)


def _mm_kernel(a_ref, w_ref, o_ref):
    o_ref[...] = _dot(a_ref[...], w_ref[...]).astype(o_ref.dtype)


def matmul(a, w, out_dtype, tm, tn):
    m, k = a.shape
    n = w.shape[1]
    blk = _nbytes((tm, k), a.dtype) + _nbytes((k, tn), w.dtype) + _nbytes((tm, tn), out_dtype)
    return pl.pallas_call(
        _mm_kernel,
        grid=(m // tm, n // tn),
        in_specs=[pl.BlockSpec((tm, k), lambda i, j: (i, 0)),
                  pl.BlockSpec((k, tn), lambda i, j: (0, j))],
        out_specs=pl.BlockSpec((tm, tn), lambda i, j: (i, j)),
        out_shape=jax.ShapeDtypeStruct((m, n), out_dtype),
        compiler_params=_cparams(("parallel", "parallel"), blk, _nbytes((tm, tn), F32)),
    )(a, w)


def _rwkv_prep_kernel(x_ref, xp_ref, xn_ref, mu_ref, wd_ref, wa_ref, wg_ref, w0_ref, a0_ref,
                      kk_ref, ka_ref, rk_ref, bo_ref,
                      r_out, v_out, nkk_out, g_out, bonus_out, w_out, kd_out, z_out):
    i = pl.program_id(0)
    lat_tiles = NL // PREP_TM
    seq_tiles = T // PREP_TM
    is_ctx = i >= lat_tiles
    first = jnp.logical_or(is_ctx, i % seq_tiles == 0)
    last = jnp.logical_or(is_ctx, i % seq_tiles == seq_tiles - 1)
    x = x_ref[...]
    prev_row = jnp.where(first, 0.0, xp_ref[7:8, :])
    next_row = jnp.where(last, 0.0, xn_ref[0:1, :])
    rows = lax.broadcasted_iota(I32, x.shape, 0)
    xp = jnp.where(rows == 0, prev_row, pltpu.roll(x, 1, axis=0))
    xn = jnp.where(rows == PREP_TM - 1, next_row, pltpu.roll(x, PREP_TM - 1, axis=0))
    ps = x + mu_ref[...] * (0.5 * (xp + xn) - x)

    bo = bo_ref[...]
    r = ps[:, 0:RW]
    k = ps[:, RW:2 * RW]
    v = ps[:, 2 * RW:3 * RW]
    dec_in = ps[:, 3 * RW:3 * RW + 128]
    icl_in = ps[:, 3 * RW + 128:3 * RW + 256]
    gate_in = ps[:, 3 * RW + 256:3 * RW + 384]
    lora_dec = _dot(jnp.tanh(dec_in).astype(BF16), wd_ref[...])
    lora_icl = _dot(icl_in.astype(BF16), wa_ref[...])
    g = _dot(_sigmoid(gate_in).astype(BF16), wg_ref[...])

    kk = k * kk_ref[...]
    kkn = kk * lax.rsqrt(_group_sum(kk * kk, bo) + 1e-12)
    ksum = jnp.zeros_like(k)
    for d in range(2):
        sl = slice(d * RW, (d + 1) * RW)
        w_log = -_softplus(-(w0_ref[:, sl] + lora_dec[:, sl])) - 0.5
        decay = jnp.exp(-jnp.exp(w_log))
        a = _sigmoid(a0_ref[:, sl] + lora_icl[:, sl])
        k_d = k * (1.0 + (a - 1.0) * ka_ref[...])
        ksum = ksum + k_d
        w_out[d] = decay
        kd_out[d] = k_d
        z_out[d] = kkn * a
    r_out[...] = r
    v_out[...] = v
    nkk_out[...] = -kkn
    g_out[...] = g
    bonus_out[...] = _group_sum(r * (0.5 * ksum) * rk_ref[...], bo) * v


def rwkv_prep(pr, mu, wd_blk, wa_blk, wg, w0, a0, k_k, k_a, r_k, bo):
    n = pr.shape[0]
    nt = n // PREP_TM
    cols = pr.shape[1]
    row = lambda c: pl.BlockSpec((1, c), lambda i: (0, 0))
    full = lambda s: pl.BlockSpec(s, lambda i: (0, 0))
    o1 = pl.BlockSpec((PREP_TM, RW), lambda i: (i, 0))
    o2 = pl.BlockSpec((2, PREP_TM, RW), lambda i: (0, i, 0))
    s1 = jax.ShapeDtypeStruct((n, RW), F32)
    s2 = jax.ShapeDtypeStruct((2, n, RW), F32)
    blk = _nbytes((PREP_TM, cols), F32) + 11 * _nbytes((PREP_TM, RW), F32)
    return pl.pallas_call(
        _rwkv_prep_kernel,
        grid=(nt,),
        in_specs=[pl.BlockSpec((PREP_TM, cols), lambda i: (i, 0)),
                  pl.BlockSpec((8, cols), lambda i: (jnp.maximum(i * (PREP_TM // 8) - 1, 0), 0)),
                  pl.BlockSpec((8, cols), lambda i: (jnp.minimum((i + 1) * (PREP_TM // 8), n // 8 - 1), 0)),
                  row(cols), full((128, 2 * RW)), full((128, 2 * RW)), full((128, RW)),
                  row(2 * RW), row(2 * RW), row(RW), row(RW), row(RW), full((128, 128))],
        out_specs=[o1, o1, o1, o1, o1, o2, o2, o2],
        out_shape=[s1, s1, s1, s1, s1, s2, s2, s2],
        compiler_params=_cparams(("parallel",), blk, 12 * _nbytes((PREP_TM, cols), F32)),
    )(pr, pr, pr, mu, wd_blk, wa_blk, wg, w0, a0, k_k, k_a, r_k, bo)


NCHAIN = B * NPAIR


def _rwkv_scan_kernel(*refs):
    nin = 2 * B * 6
    vec_refs = refs[:nin]
    s0_ref, bo_ref, m1_ref, yf_ref, yb_ref, sfin_ref, s_scr, yt_scr = refs[nin:]
    j = pl.program_id(0)
    nb = pl.num_programs(0)

    @pl.when(j == 0)
    def _():
        s_scr[...] = s0_ref[...]

    yt_scr[...] = jnp.zeros_like(yt_scr)
    bo = bo_ref[...]
    m1 = m1_ref[...]
    lane_mod = lax.broadcasted_iota(I32, (HD, LANES), 1) % HD
    shape3 = (NCHAIN, HD, LANES)
    half = NCHAIN // 2

    def bsum(x):
        x2 = jnp.concatenate([x[:half], x[half:]], axis=-1).reshape(half * HD, 2 * LANES).astype(BF16)
        y2 = _dot(x2, bo).reshape(half, HD, 2 * LANES)
        return jnp.concatenate([y2[..., :LANES], y2[..., LANES:]], axis=0)

    sub8 = 8
    ngroups = SCAN_TB // sub8

    def group(gi, carry):
        bases = (pl.multiple_of(gi * sub8, sub8), pl.multiple_of((ngroups - 1 - gi) * sub8, sub8))
        for u in range(sub8):
            for d in range(2):
                base = bases[d]
                off = u if d == 0 else sub8 - 1 - u
                t = base + off

                def row(which):
                    parts = []
                    for b in range(B):
                        ref = vec_refs[(d * B + b) * 6 + which]
                        for p in range(NPAIR):
                            parts.append(ref[pl.ds(base, sub8), p * LANES:(p + 1) * LANES][off:off + 1]
                                         .reshape(1, 1, LANES))
                    return jnp.broadcast_to(jnp.concatenate(parts, axis=0), shape3)

                st = s_scr[d]
                sa = bsum(st * row(2))
                vcol = bsum(m1 * row(1))
                st = st * row(3) + sa * row(5) + vcol * row(4)
                s_scr[d] = st
                ycol = bsum(st * row(0))
                yt_scr[d] = jnp.where(lane_mod == t, ycol, yt_scr[d])
        return carry

    lax.fori_loop(0, ngroups, group, 0)

    lane = lax.broadcasted_iota(I32, (HD, LANES), 1)
    sub = lax.broadcasted_iota(I32, (HD, LANES), 0)
    pick = jnp.where(lane_mod == sub, 1.0, 0.0).astype(BF16)
    for d, y_ref in ((0, yf_ref), (1, yb_ref)):
        for b in range(B):
            for p in range(NPAIR):
                yt = yt_scr[d, b * NPAIR + p]
                z = jnp.concatenate([jnp.where(lane < HD, yt, 0.0), jnp.where(lane >= HD, yt, 0.0)], axis=0)
                zh, zl = _split2(z)
                y_ref[b, :, p * LANES:(p + 1) * LANES] = _dot_nt(pick, zh) + _dot_nt(pick, zl)

    @pl.when(j == nb - 1)
    def _():
        sfin_ref[...] = s_scr[...]


def rwkv_scan(r, v, nkk, w, kd, z, s0, bo2, m1, row0, seq):
    nb = seq // SCAN_TB
    base = row0 // SCAN_TB

    def blk_idx(d, b):
        if d == 0:
            return lambda j: base + b * nb + j
        return lambda j: base + b * nb + nb - 1 - j

    def shared(idx):
        return pl.BlockSpec((SCAN_TB, RW), lambda j: (idx(j), 0))

    def per_dir(d, idx):
        return pl.BlockSpec((None, SCAN_TB, RW), lambda j: (d, idx(j), 0))

    in_specs, args = [], []
    for d in range(2):
        for b in range(B):
            idx = blk_idx(d, b)
            in_specs += [shared(idx), shared(idx), shared(idx), per_dir(d, idx), per_dir(d, idx), per_dir(d, idx)]
            args += [r, v, nkk, w, kd, z]
    st_shape = (2, NCHAIN, HD, LANES)
    st_spec = pl.BlockSpec(st_shape, lambda j: (0, 0, 0, 0))
    in_specs += [st_spec, pl.BlockSpec((2 * LANES, 2 * LANES), lambda j: (0, 0)),
                 pl.BlockSpec((HD, LANES), lambda j: (0, 0))]
    blk = (len(args) + 2 * B) * _nbytes((SCAN_TB, RW), F32) + 2 * _nbytes(st_shape, F32)
    return pl.pallas_call(
        _rwkv_scan_kernel,
        grid=(nb,),
        in_specs=in_specs,
        out_specs=[pl.BlockSpec((B, SCAN_TB, RW), lambda j: (0, j, 0)),
                   pl.BlockSpec((B, SCAN_TB, RW), lambda j: (0, nb - 1 - j, 0)),
                   st_spec],
        out_shape=[jax.ShapeDtypeStruct((B, seq, RW), F32),
                   jax.ShapeDtypeStruct((B, seq, RW), F32),
                   jax.ShapeDtypeStruct(st_shape, F32)],
        scratch_shapes=[pltpu.VMEM(st_shape, F32), pltpu.VMEM(st_shape, F32)],
        compiler_params=_cparams(("arbitrary",), blk, 2 * _nbytes(st_shape, F32)),
    )(*args, s0, bo2, m1)


def _na_norm(x, gain, bo):
    xf = x.astype(F32)
    ms = _group_sum(xf * xf, bo) * (1.0 / HD)
    return xf * lax.rsqrt(ms + NORM_EPS) * gain


def _na_kernel(q_ref, k_ref, v_ref, qc_ref, kc_ref, vc_ref, bias_ref, qg_ref, kg_ref, bo_ref,
               o_ref, oc_ref, qn_scr, kn_scr):
    bo = bo_ref[...]
    scale = HD ** -0.5
    nchunk = 8
    rows = T // nchunk

    def norm_chunk(c, carry):
        sl = pl.ds(pl.multiple_of(c * rows, rows), rows)
        qn_scr[sl, :] = (_na_norm(q_ref[sl, :], qg_ref[...], bo) * scale).astype(BF16)
        kn_scr[sl, :] = _na_norm(k_ref[sl, :], kg_ref[...], bo).astype(BF16)
        return carry

    lax.fori_loop(0, nchunk, norm_chunk, 0)
    kc = _na_norm(kc_ref[...], kg_ref[...], bo).astype(BF16)
    qc = (_na_norm(qc_ref[...], qg_ref[...], bo) * scale).astype(BF16)
    vc = vc_ref[...]

    def two_heads(q2):
        lane = lax.broadcasted_iota(I32, q2.shape, 1)
        zero = jnp.zeros_like(q2)
        return jnp.concatenate([jnp.where(lane < HD, q2, zero), jnp.where(lane >= HD, q2, zero)], axis=0)

    def merge_heads(o, n):
        lane = lax.broadcasted_iota(I32, (n, LANES), 1)
        return jnp.where(lane < HD, o[0:n], o[n:2 * n])

    s = _dot_nt(two_heads(qc), kc)
    p = jnp.exp(s - jnp.max(s, axis=-1, keepdims=True))
    o = _dot(p.astype(BF16), vc) / jnp.sum(p, axis=-1, keepdims=True)
    oc_ref[...] = merge_heads(o, C).astype(BF16)

    band = NA_KH * GRID_W

    def row_step(r, carry):
        r0 = jnp.clip(r - NA_KH // 2, 0, GRID_H - NA_KH)
        dr = r - r0
        q2 = qn_scr[pl.ds(pl.multiple_of(r * GRID_W, GRID_W), GRID_W), :]
        ksl = pl.ds(pl.multiple_of(r0 * GRID_W, GRID_W), band)
        lhs = two_heads(q2)
        bias = jnp.concatenate([bias_ref[0, dr], bias_ref[1, dr]], axis=0)
        s_w = _dot_nt(lhs, kn_scr[ksl, :]) + bias
        s_c = _dot_nt(lhs, kc)
        m = jnp.maximum(jnp.max(s_w, axis=-1, keepdims=True), jnp.max(s_c, axis=-1, keepdims=True))
        p_w = jnp.exp(s_w - m)
        p_c = jnp.exp(s_c - m)
        l = jnp.sum(p_w, axis=-1, keepdims=True) + jnp.sum(p_c, axis=-1, keepdims=True)
        o = (_dot(p_w.astype(BF16), v_ref[ksl, :]) + _dot(p_c.astype(BF16), vc)) / l
        o_ref[pl.ds(pl.multiple_of(r * GRID_W, GRID_W), GRID_W), :] = merge_heads(o, GRID_W).astype(BF16)
        return carry

    lax.fori_loop(0, GRID_H, row_step, 0, unroll=4)


def na_attention(pna, bias_tab, qg, kg, bo):
    hp = NA_W // LANES
    ctx0 = NL // C

    def lat(col0):
        return pl.BlockSpec((T, LANES), lambda b, p: (b, col0 + p))

    def ctx(col0):
        return pl.BlockSpec((C, LANES), lambda b, p: (ctx0 + b, col0 + p))

    vec = pl.BlockSpec((1, LANES), lambda b, p: (0, 0))
    blk = 3 * _nbytes((T, LANES), BF16) + _nbytes((2, NA_KH, GRID_W, NA_KH * GRID_W), F32) + _nbytes((T, LANES), BF16)
    return pl.pallas_call(
        _na_kernel,
        grid=(B, hp),
        in_specs=[lat(0), lat(hp), lat(2 * hp), ctx(0), ctx(hp), ctx(2 * hp),
                  pl.BlockSpec((2, NA_KH, GRID_W, NA_KH * GRID_W), lambda b, p: (p, 0, 0, 0)),
                  vec, vec, pl.BlockSpec((LANES, LANES), lambda b, p: (0, 0))],
        out_specs=[pl.BlockSpec((T, LANES), lambda b, p: (b, p)),
                   pl.BlockSpec((C, LANES), lambda b, p: (b, p))],
        out_shape=[jax.ShapeDtypeStruct((NL, NA_W), BF16), jax.ShapeDtypeStruct((NC, NA_W), BF16)],
        scratch_shapes=[pltpu.VMEM((T, LANES), BF16), pltpu.VMEM((T, LANES), BF16)],
        compiler_params=_cparams(("parallel", "parallel"), blk, 2 * _nbytes((T, LANES), BF16)),
    )(pna, pna, pna, pna, pna, pna, bias_tab, qg, kg, bo)


def _na_bias_table(rpb):
    nh, nr, nc = rpb.shape
    w = GRID_W
    span = 2 * w - 1
    pad = w - NA_KW
    f = jnp.pad(rpb, ((0, 0), (0, 0), (pad, span - nc - pad)))
    flat = jnp.pad(jnp.tile(f, (1, 1, w)), ((0, 0), (0, 0), (0, w)))
    toep = jnp.flip(flat.reshape(nh, nr, w, 2 * w), axis=2)[..., :w]
    q = jnp.arange(w)[:, None]
    kc = jnp.arange(w)[None, :]
    cs = jnp.clip(q - NA_KW // 2, 0, w - NA_KW)
    toep = jnp.where((kc >= cs) & (kc < cs + NA_KW), toep, -1e30)
    tab = jnp.stack([toep[:, NA_KH - 1 - dr:2 * NA_KH - 1 - dr] for dr in range(NA_KH)], axis=1)
    return tab.transpose(0, 1, 3, 2, 4).reshape(nh, NA_KH, w, NA_KH * w)


def _l0_out_kernel(yf_ref, yb_ref, bonus_ref, g_ref, bna_ref, x_ref, wa_ref, wb_ref,
                   lng_ref, lnb_ref, gt_ref, n2_ref, sh_ref, sc_ref, bo_ref, xo_ref, ho_ref):
    bo = bo_ref[...]
    y = yf_ref[...] + yb_ref[...]
    mu = _group_sum(y, bo) * (1.0 / HD)
    dlt = y - mu
    var = _group_sum(dlt * dlt, bo) * (1.0 / HD)
    yn = dlt * lax.rsqrt(var + RWKV_GN_EPS) * lng_ref[...] + lnb_ref[...]
    a = ((yn + bonus_ref[...]) * g_ref[...]).astype(BF16)
    acc = _dot(a, wa_ref[...]) + _dot(bna_ref[...], wb_ref[...])
    xo = x_ref[...] + gt_ref[...] * acc
    xo_ref[...] = xo
    ho_ref[...] = _rms_mod(xo, n2_ref[...], sh_ref[...], sc_ref[...]).astype(BF16)


def l0_out(yf, yb, bonus, g, bna, x, wa, wb, lnx_g, lnx_b, gt, norm2, sh, sc, bo):
    n = x.shape[0]
    half = pl.BlockSpec((TM, RW), lambda i: (i, 0))
    fullrow = pl.BlockSpec((TM, D), lambda i: (i, 0))
    wspec = pl.BlockSpec((RW, D), lambda i: (0, 0))
    vec = pl.BlockSpec((None, 1, D), lambda i: (_seg(i), 0, 0))
    r1 = pl.BlockSpec((1, RW), lambda i: (0, 0))
    blk = (4 * _nbytes((TM, RW), F32) + _nbytes((TM, RW), BF16) + 2 * _nbytes((TM, D), F32)
           + _nbytes((TM, D), BF16) + 2 * _nbytes((RW, D), BF16))
    return pl.pallas_call(
        _l0_out_kernel,
        grid=(n // TM,),
        in_specs=[half, half, half, half, half, fullrow, wspec, wspec, r1, r1, vec,
                  pl.BlockSpec((1, D), lambda i: (0, 0)), vec, vec,
                  pl.BlockSpec((LANES, LANES), lambda i: (0, 0))],
        out_specs=[fullrow, fullrow],
        out_shape=[jax.ShapeDtypeStruct((n, D), F32), jax.ShapeDtypeStruct((n, D), BF16)],
        compiler_params=_cparams(("parallel",), blk, 4 * _nbytes((TM, D), F32)),
    )(yf, yb, bonus, g, bna, x, wa, wb, lnx_g, lnx_b, gt, norm2, sh, sc, bo)


FFN_TF = 512


def _ffn_kernel(h_ref, x_ref, wg_ref, wu_ref, wd_ref, gt_ref, n_ref, sh_ref, sc_ref, xo_ref, ho_ref, acc):
    f = pl.program_id(1)

    @pl.when(f == 0)
    def _():
        acc[...] = jnp.zeros_like(acc)

    h = h_ref[...]
    act = (_silu(_dot(h, wg_ref[...])) * _dot(h, wu_ref[...])).astype(BF16)
    acc[...] += _dot(act, wd_ref[...])

    @pl.when(f == pl.num_programs(1) - 1)
    def _():
        xo = x_ref[...] + gt_ref[...] * acc[...]
        xo_ref[...] = xo
        ho_ref[...] = _rms_mod(xo, n_ref[...], sh_ref[...], sc_ref[...]).astype(BF16)


def ffn(h, x, wg, wu, wd, gt, norm_next, sh_next, sc_next):
    n = x.shape[0]
    rowblk = pl.BlockSpec((TM, D), lambda i, f: (i, 0))
    vec = pl.BlockSpec((None, 1, D), lambda i, f: (_seg(i), 0, 0))
    blk = (_nbytes((TM, D), BF16) * 2 + _nbytes((TM, D), F32) * 2
           + 3 * _nbytes((D, FFN_TF), BF16))
    return pl.pallas_call(
        _ffn_kernel,
        grid=(n // TM, FFN // FFN_TF),
        in_specs=[rowblk, rowblk,
                  pl.BlockSpec((D, FFN_TF), lambda i, f: (0, f)),
                  pl.BlockSpec((D, FFN_TF), lambda i, f: (0, f)),
                  pl.BlockSpec((FFN_TF, D), lambda i, f: (f, 0)),
                  vec, pl.BlockSpec((1, D), lambda i, f: (0, 0)), vec, vec],
        out_specs=[rowblk, rowblk],
        out_shape=[jax.ShapeDtypeStruct((n, D), F32), jax.ShapeDtypeStruct((n, D), BF16)],
        scratch_shapes=[pltpu.VMEM((TM, D), F32)],
        compiler_params=_cparams(("parallel", "arbitrary"), blk, 3 * _nbytes((TM, D), F32)),
    )(h, x, wg, wu, wd, gt, norm_next, sh_next, sc_next)


def _gla_kernel(k_ref, v_ref, q_ref, lr_ref, w2_ref, gb_ref, s0_ref, o_ref, sfin_ref, s_scr, *, rev, nchunks):
    j = pl.program_id(2)

    @pl.when(j == 0)
    def _():
        s_scr[...] = s0_ref[...]

    L = GLA_CHUNK
    row = lax.broadcasted_iota(I32, (L, L), 0)
    col = lax.broadcasted_iota(I32, (L, L), 1)
    tri = jnp.where((col >= row) if rev else (col <= row), 1.0, 0.0).astype(BF16)
    nsb = L // GLA_SUB

    def chunk(ci, st):
        c = nchunks - 1 - ci if rev else ci
        sl = slice(c * L, (c + 1) * L)
        k = k_ref[sl, :].astype(F32)
        v = v_ref[sl, :]
        q = q_ref[sl, :].astype(F32) * (GLA_DK ** -0.5)
        pre = _dot3(lr_ref[sl, :], w2_ref[...]) + gb_ref[...]
        g = (jnp.minimum(pre, 0.0) - jnp.log1p(jnp.exp(-jnp.abs(pre)))) * (1.0 / GLA_NORMALIZER)
        b = _dot_exact_lhs(tri, g)
        bend = b[0:1] if rev else b[L - 1:L]
        o_inter = _dot_nt((q * jnp.exp(b)).astype(BF16), st.astype(BF16))
        blocks = []
        for i in range(nsb):
            if rev:
                r_lo, r_hi = L - GLA_SUB * (i + 1), L - GLA_SUB * i
                k_lo, k_hi = r_lo, L
                ref = b[r_hi:r_hi + 1] if i > 0 else None
            else:
                r_lo, r_hi = GLA_SUB * i, GLA_SUB * (i + 1)
                k_lo, k_hi = 0, r_hi
                ref = b[r_lo - 1:r_lo] if i > 0 else None
            bq = b[r_lo:r_hi]
            bk = b[k_lo:k_hi]
            if ref is None:
                qe = q[r_lo:r_hi] * jnp.exp(bq)
                ke = k[k_lo:k_hi] * jnp.exp(-bk)
            else:
                qe = q[r_lo:r_hi] * jnp.exp(bq - ref)
                ke = k[k_lo:k_hi] * jnp.exp(ref - bk)
            att = _dot_nt(qe.astype(BF16), ke.astype(BF16))
            nk = k_hi - k_lo
            tq = lax.broadcasted_iota(I32, (GLA_SUB, nk), 0)
            sk = lax.broadcasted_iota(I32, (GLA_SUB, nk), 1)
            ok = (sk >= tq) if rev else (sk <= tq + r_lo)
            att = jnp.where(ok, att, 0.0)
            blocks.append((r_lo, _dot(att.astype(BF16), v[k_lo:k_hi])))
        blocks.sort(key=lambda t: t[0])
        o_ref[sl, :] = o_inter + jnp.concatenate([blk for _, blk in blocks], axis=0)
        kd = (k * jnp.exp(bend - b)).astype(BF16)
        return st * jnp.exp(bend) + _dot_tn(v, kd)

    st = s_scr[...]
    for ci in range(nchunks):
        st = chunk(ci, st)
    s_scr[...] = st

    @pl.when(j == pl.num_programs(2) - 1)
    def _():
        sfin_ref[...] = s_scr[...]


def gla_scan(p1, lr, w2p, gbp, s0, d, row0, seq, tb):
    rev = d == 1
    nb = seq // tb
    base = row0 // tb
    nchunks = tb // GLA_CHUNK

    def blkidx(b, j):
        return base + b * nb + (nb - 1 - j if rev else j)

    kcol = GLA_KW // GLA_DK
    st_spec = pl.BlockSpec((None, None, GLA_DV, GLA_DK), lambda b, h, j: (b, h, 0, 0))
    blk = (2 * _nbytes((tb, GLA_DK), BF16) + _nbytes((tb, GLA_DV), BF16) + _nbytes((tb, LANES), F32)
           + _nbytes((tb, GLA_DV), F32) + 2 * _nbytes((GLA_DV, GLA_DK), F32))
    return pl.pallas_call(
        functools.partial(_gla_kernel, rev=rev, nchunks=nchunks),
        grid=(B, GLA_H, nb),
        in_specs=[pl.BlockSpec((tb, GLA_DK), lambda b, h, j: (blkidx(b, j), h)),
                  pl.BlockSpec((tb, GLA_DV), lambda b, h, j: (blkidx(b, j), GLA_KW // GLA_DV + h)),
                  pl.BlockSpec((tb, GLA_DK), lambda b, h, j: (blkidx(b, j), (GLA_KW + GLA_VW) // GLA_DK + h)),
                  pl.BlockSpec((tb, LANES), lambda b, h, j: (blkidx(b, j), 0)),
                  pl.BlockSpec((None, LANES, GLA_DK), lambda b, h, j: (d, 0, h)),
                  pl.BlockSpec((None, 1, GLA_DK), lambda b, h, j: (d, 0, h)),
                  st_spec],
        out_specs=[pl.BlockSpec((tb, GLA_DV), lambda b, h, j: (b * nb + (nb - 1 - j if rev else j), h)),
                   st_spec],
        out_shape=[jax.ShapeDtypeStruct((B * seq, GLA_VW), F32),
                   jax.ShapeDtypeStruct((B, GLA_H, GLA_DV, GLA_DK), F32)],
        scratch_shapes=[pltpu.VMEM((GLA_DV, GLA_DK), F32)],
        compiler_params=_cparams(("parallel", "parallel", "arbitrary"), blk, 8 * _nbytes((GLA_DV, GLA_DK), F32)),
    )(p1, p1, p1, lr, w2p, gbp, s0)


def _l1_out_kernel(of_ref, ob_ref, gate_ref, x_ref, w_ref, on_ref, gt_ref, n2_ref, sh_ref, sc_ref, rt_ref,
                   xo_ref, hu_ref, route_ref):
    o = of_ref[...] + ob_ref[...]
    heads = []
    for h in range(GLA_H):
        oh = o[:, h * GLA_DV:(h + 1) * GLA_DV]
        ms = jnp.mean(oh * oh, axis=-1, keepdims=True)
        heads.append(oh * lax.rsqrt(ms + NORM_EPS) * on_ref[...])
    og = (jnp.concatenate(heads, axis=-1) * _silu(gate_ref[...].astype(F32))).astype(BF16)
    xo = x_ref[...] + gt_ref[...] * _dot(og, w_ref[...])
    xo_ref[...] = xo
    h4 = _rms_mod(xo, n2_ref[...], sh_ref[...], sc_ref[...])

    logits = _dot3(h4, rt_ref[...])
    lane = lax.broadcasted_iota(I32, logits.shape, 1)
    neg = -jnp.inf
    lg = jnp.where(lane < NE, logits, neg)
    m1 = jnp.max(lg, axis=-1, keepdims=True)
    i1 = jnp.min(jnp.where(lg == m1, lane, LANES), axis=-1, keepdims=True)
    lg2 = jnp.where(lane == i1, neg, lg)
    m2 = jnp.max(lg2, axis=-1, keepdims=True)
    i2 = jnp.min(jnp.where(lg2 == m2, lane, LANES), axis=-1, keepdims=True)
    e2 = jnp.exp(m2 - m1)
    w1 = 1.0 / (1.0 + e2)
    w2 = e2 / (1.0 + e2)
    route = jnp.where(lane == 0, i1.astype(F32),
                      jnp.where(lane == 1, i2.astype(F32),
                                jnp.where(lane == 2, w1, jnp.where(lane == 3, w2, 0.0))))
    route_ref[...] = route
    hu_ref[...] = h4


def l1_out(of, ob, p1, x, w_out, o_norm, gt, norm2, sh, sc, router_p):
    n = NL
    tm = K10_TM
    seg = lambda i: i // (T // tm)
    rowblk = pl.BlockSpec((tm, D), lambda i: (i, 0))
    vec = pl.BlockSpec((None, 1, D), lambda i: (seg(i), 0, 0))
    blk = (3 * _nbytes((tm, D), F32) + _nbytes((tm, D), BF16) + _nbytes((D, D), BF16)
           + _nbytes((D, LANES), F32) + 2 * _nbytes((tm, D), F32))
    return pl.pallas_call(
        _l1_out_kernel,
        grid=(n // tm,),
        in_specs=[rowblk, rowblk,
                  pl.BlockSpec((tm, GLA_VW), lambda i: (i, (2 * GLA_KW + GLA_VW) // GLA_VW)),
                  rowblk, pl.BlockSpec((D, D), lambda i: (0, 0)),
                  pl.BlockSpec((1, GLA_DV), lambda i: (0, 0)), vec,
                  pl.BlockSpec((1, D), lambda i: (0, 0)), vec, vec,
                  pl.BlockSpec((D, LANES), lambda i: (0, 0))],
        out_specs=[rowblk, rowblk, pl.BlockSpec((tm, LANES), lambda i: (i, 0))],
        out_shape=[jax.ShapeDtypeStruct((n, D), F32), jax.ShapeDtypeStruct((n, D), F32),
                   jax.ShapeDtypeStruct((n, LANES), F32)],
        compiler_params=_cparams(("parallel",), blk, 4 * _nbytes((tm, D), F32)),
    )(of, ob, p1, x, w_out, o_norm, gt, norm2, sh, sc, router_p)


def _dispatch_kernel(pos_ref, h_ref, init_ref, xs_ref, sem):
    del init_ref
    base = pl.program_id(0) * TM

    def copy(r, slot):
        dst = pos_ref[slot * NL + base + r]
        return pltpu.make_async_copy(h_ref.at[pl.ds(r, 1)], xs_ref.at[pl.ds(dst, 1)], sem)

    def issue(r, carry):
        copy(r, 0).start()
        copy(r, 1).start()
        return carry

    def drain(r, carry):
        copy(r, 0).wait()
        copy(r, 1).wait()
        return carry

    lax.fori_loop(0, TM, issue, 0)
    lax.fori_loop(0, TM, drain, 0)


def moe_dispatch(pos, h4u, xs_init):
    return pl.pallas_call(
        _dispatch_kernel,
        grid_spec=pltpu.PrefetchScalarGridSpec(
            num_scalar_prefetch=1,
            grid=(NL // TM,),
            in_specs=[pl.BlockSpec((TM, D), lambda i, pos: (i, 0)),
                      pl.BlockSpec(memory_space=pl.ANY)],
            out_specs=pl.BlockSpec(memory_space=pl.ANY),
            scratch_shapes=[pltpu.SemaphoreType.DMA(())]),
        out_shape=jax.ShapeDtypeStruct((MOE_ROWS, D), F32),
        input_output_aliases={2: 0},
        compiler_params=pltpu.CompilerParams(dimension_semantics=("arbitrary",)),
    )(pos, h4u, xs_init)


def _moe_kernel(ie_ref, is_ref, in_ref, nit_ref, *refs):
    x_refs = refs[0:MOE_NSUB]
    wg_ref, wu_ref, wd_ref, ys_ref, xb, acc, sem = refs[MOE_NSUB:]
    i = pl.program_id(0)
    f = pl.program_id(1)
    valid = i < nit_ref[0]

    @pl.when(jnp.logical_and(valid, f == 0))
    def _():
        for s in range(MOE_NSUB):
            xb[s * MOE_SUB:(s + 1) * MOE_SUB, :] = x_refs[s][...].astype(BF16)
        acc[...] = jnp.zeros_like(acc)

    for n in range(1, MOE_NSUB + 1):
        @pl.when(jnp.logical_and(valid, in_ref[jnp.minimum(i, nit_ref[0] - 1)] == n))
        def _():
            rows = n * MOE_SUB
            x = xb[0:rows, :]
            act = (_silu(_dot(x, wg_ref[...].astype(BF16))) * _dot(x, wu_ref[...].astype(BF16))).astype(BF16)
            acc[0:rows, :] += _dot(act, wd_ref[...].astype(BF16))

    @pl.when(jnp.logical_and(valid, f == pl.num_programs(1) - 1))
    def _():
        start = is_ref[i]
        nsub = in_ref[i]
        for s in range(MOE_NSUB):
            @pl.when(s < nsub)
            def _():
                dst = ys_ref.at[pl.ds(pl.multiple_of((start + s) * MOE_SUB, MOE_SUB), MOE_SUB)]
                cp = pltpu.make_async_copy(acc.at[pl.ds(s * MOE_SUB, MOE_SUB)], dst, sem)
                cp.start()
                cp.wait()

    @pl.when(jnp.logical_and(i == pl.num_programs(0) - 1, f == pl.num_programs(1) - 1))
    def _():
        used = nit_ref[1]
        acc[0:MOE_SUB, :] = jnp.zeros((MOE_SUB, D), F32)
        for u in range(NE):
            @pl.when(used + u < MOE_ROWS // MOE_SUB)
            def _():
                dst = ys_ref.at[pl.ds(pl.multiple_of((used + u) * MOE_SUB, MOE_SUB), MOE_SUB)]
                cp = pltpu.make_async_copy(acc.at[pl.ds(0, MOE_SUB)], dst, sem)
                cp.start()
                cp.wait()


def moe_ffn(item_e, item_s, item_n, n_items, xs, wg, wu, wd):
    nf = EDIM // MOE_TF
    rows = MOE_SUB * MOE_NSUB

    def item(i, nit):
        return jnp.minimum(i, nit[0] - 1)

    def sub_idx(s):
        def idx(i, f, ie, ist, inn, nit):
            ii = item(i, nit)
            return (ist[ii] + jnp.minimum(s, inn[ii] - 1), 0)
        return idx

    x_specs = [pl.BlockSpec((MOE_SUB, D), sub_idx(s)) for s in range(MOE_NSUB)]
    w_in = lambda i, f, ie, ist, inn, nit: (ie[item(i, nit)], 0, f)
    w_dn = lambda i, f, ie, ist, inn, nit: (ie[item(i, nit)], f, 0)
    blk = MOE_NSUB * _nbytes((MOE_SUB, D), F32) + 3 * _nbytes((D, MOE_TF), F32)
    scratch = (_nbytes((rows, D), BF16) + _nbytes((rows, D), F32) + 6 * _nbytes((D, MOE_TF), BF16)
               + 3 * _nbytes((rows, MOE_TF), F32))
    return pl.pallas_call(
        _moe_kernel,
        grid_spec=pltpu.PrefetchScalarGridSpec(
            num_scalar_prefetch=4,
            grid=(MOE_ITEMS, nf),
            in_specs=x_specs + [pl.BlockSpec((None, D, MOE_TF), w_in),
                                pl.BlockSpec((None, D, MOE_TF), w_in),
                                pl.BlockSpec((None, MOE_TF, D), w_dn)],
            out_specs=pl.BlockSpec(memory_space=pl.ANY),
            scratch_shapes=[pltpu.VMEM((rows, D), BF16), pltpu.VMEM((rows, D), F32),
                            pltpu.SemaphoreType.DMA(())]),
        out_shape=jax.ShapeDtypeStruct((MOE_ROWS, D), F32),
        compiler_params=_cparams(("arbitrary", "arbitrary"), blk, scratch),
    )(item_e, item_s, item_n, n_items, *([xs] * MOE_NSUB), wg, wu, wd)


COMB_TM = 256


def _combine_kernel(pos_ref, x_ref, route_ref, gt_ref, ys_ref, o_ref, g0, g1, sem):
    base = pl.program_id(0) * COMB_TM

    def copy(r, slot, buf):
        src = pos_ref[slot * NL + base + r]
        return pltpu.make_async_copy(ys_ref.at[pl.ds(src, 1)], buf.at[pl.ds(r, 1)], sem)

    def issue(r, carry):
        copy(r, 0, g0).start()
        copy(r, 1, g1).start()
        return carry

    def drain(r, carry):
        copy(r, 0, g0).wait()
        copy(r, 1, g1).wait()
        return carry

    lax.fori_loop(0, COMB_TM, issue, 0)
    lax.fori_loop(0, COMB_TM, drain, 0)
    rt = route_ref[...]
    y = rt[:, 2:3] * g0[...] + rt[:, 3:4] * g1[...]
    o_ref[...] = x_ref[...] + gt_ref[...] * y


def moe_combine(pos, x, route, gt, ys):
    rowblk = pl.BlockSpec((COMB_TM, D), lambda i, pos: (i, 0))
    return pl.pallas_call(
        _combine_kernel,
        grid_spec=pltpu.PrefetchScalarGridSpec(
            num_scalar_prefetch=1,
            grid=(NL // COMB_TM,),
            in_specs=[rowblk,
                      pl.BlockSpec((COMB_TM, LANES), lambda i, pos: (i, 0)),
                      pl.BlockSpec((None, 1, D), lambda i, pos: (i // (T // COMB_TM), 0, 0)),
                      pl.BlockSpec(memory_space=pl.ANY)],
            out_specs=rowblk,
            scratch_shapes=[pltpu.VMEM((COMB_TM, D), F32), pltpu.VMEM((COMB_TM, D), F32),
                            pltpu.SemaphoreType.DMA(())]),
        out_shape=jax.ShapeDtypeStruct((NL, D), F32),
        compiler_params=_cparams(("arbitrary",), 2 * _nbytes((COMB_TM, D), F32), 2 * _nbytes((COMB_TM, D), F32)),
    )(pos, x, route, gt, ys)


def _block_ones(n=LANES):
    i = jnp.arange(n)
    return (i[:, None] // HD == i[None, :] // HD).astype(BF16)


def _diag_pick():
    return (jnp.arange(LANES)[None, :] % HD == jnp.arange(HD)[:, None]).astype(F32)


def _blockdiag2(w):
    z = jnp.zeros_like(w[0])
    return jnp.concatenate([jnp.concatenate([w[0], z], axis=1), jnp.concatenate([z, w[1]], axis=1)], axis=0)


def _route_tables(route):
    e = jnp.concatenate([route[:, 0], route[:, 1]]).astype(I32)
    onehot = (e[:, None] == jnp.arange(NE, dtype=I32)[None, :]).astype(I32)
    cum = jnp.cumsum(onehot, axis=0)
    rank = jnp.sum(onehot * cum, axis=1) - 1
    counts = cum[-1]
    nsub = (counts + MOE_SUB - 1) // MOE_SUB
    sub_start = jnp.cumsum(nsub) - nsub
    pos = (sub_start * MOE_SUB)[e] + rank
    nitem_e = (nsub + MOE_NSUB - 1) // MOE_NSUB
    item_end = jnp.cumsum(nitem_e)
    item_start = item_end - nitem_e
    ii = jnp.arange(MOE_ITEMS, dtype=I32)
    item_e = jnp.minimum(jnp.sum((ii[:, None] >= item_end[None, :]).astype(I32), axis=1), NE - 1)
    local = ii - item_start[item_e]
    item_s = sub_start[item_e] + MOE_NSUB * local
    item_n = jnp.clip(nsub[item_e] - MOE_NSUB * local, 1, MOE_NSUB)
    n_items = jnp.stack([item_end[-1], jnp.sum(nsub)]).astype(I32)
    return pos.astype(I32), item_e.astype(I32), item_s.astype(I32), item_n.astype(I32), n_items


def kernel(x, c, ctx, c_ctx, l0_w_mod, l0_b_mod, l0_norm1, l0_norm2, l0_w_in, l0_mu_shift, l0_decay_w0, l0_decay_w2, l0_iclr_a0, l0_iclr_a2, l0_gate_g2, l0_k_k, l0_k_a, l0_r_k, l0_lnx_g, l0_lnx_b, l0_q_norm, l0_k_norm, l0_rpb, l0_w_out, l0_ffn_wg, l0_ffn_wu, l0_ffn_wd, l1_w_mod, l1_b_mod, l1_norm1, l1_norm2, l1_w_in, l1_gate_w2, l1_gate_b, l1_o_norm, l1_w_out, l1_router, l1_exp_wg, l1_exp_wu, l1_exp_wd):
    bo = _block_ones()
    m1 = _diag_pick()
    xr = jnp.concatenate([x.reshape(NL, D), ctx.reshape(NC, D)], axis=0)
    cond8 = jnp.zeros((8, D), F32).at[0:B].set(c).at[B].set(c_ctx)

    sh1, sc1, gt1, sh2, sc2, gt2 = _mods(cond8, l0_w_mod, l0_b_mod)
    l1sh1, l1sc1, l1gt1, l1sh2, l1sc2, l1gt2 = _mods(cond8, l1_w_mod, l1_b_mod)
    h = norm_mod(xr, l0_norm1, sh1, sc1)
    pr = matmul(h, l0_w_in[:, :RWKV_COLS].astype(BF16), F32, 1088, 1152)
    pna = matmul(h, l0_w_in[:, RWKV_COLS:].astype(BF16), BF16, 1088, 1024)

    r, v, nkk, g, bonus, w, kd, z = rwkv_prep(
        pr, l0_mu_shift.reshape(1, -1),
        _blockdiag2(l0_decay_w2).astype(BF16), _blockdiag2(l0_iclr_a2).astype(BF16), l0_gate_g2.astype(BF16),
        l0_decay_w0.reshape(1, -1), l0_iclr_a0.reshape(1, -1),
        l0_k_k.reshape(1, -1), l0_k_a.reshape(1, -1), l0_r_k.reshape(1, -1), bo)
    bo2 = _block_ones(2 * LANES)
    s_zero = jnp.zeros((2, NCHAIN, HD, LANES), F32)
    ycf, ycb, s_ctx = rwkv_scan(r, v, nkk, w, kd, z, s_zero, bo2, m1, NL, C)
    ylf, ylb, _ = rwkv_scan(r, v, nkk, w, kd, z, s_ctx, bo2, m1, 0, T)
    yf = jnp.concatenate([ylf.reshape(NL, RW), ycf.reshape(NC, RW)], axis=0)
    yb = jnp.concatenate([ylb.reshape(NL, RW), ycb.reshape(NC, RW)], axis=0)

    b_l, b_c = na_attention(pna, _na_bias_table(l0_rpb),
                            jnp.tile(l0_q_norm, 2).reshape(1, LANES), jnp.tile(l0_k_norm, 2).reshape(1, LANES), bo)
    bna = jnp.concatenate([b_l, b_c], axis=0)

    w_out0 = l0_w_out.astype(BF16)
    x1, h2 = l0_out(yf, yb, bonus, g, bna, xr, w_out0[:RW], w_out0[RW:],
                    l0_lnx_g.reshape(1, RW), l0_lnx_b.reshape(1, RW), gt1, l0_norm2.reshape(1, D), sh2, sc2, bo)
    x2, h3 = ffn(h2, x1, l0_ffn_wg.astype(BF16), l0_ffn_wu.astype(BF16), l0_ffn_wd.astype(BF16),
                 gt2, l1_norm1.reshape(1, D), l1sh1, l1sc1)

    st = GLA_KW + GLA_VW
    w1 = jnp.concatenate([l1_w_in[:, :st], l1_w_in[:, st + 2 * GLA_RANK:]], axis=1).astype(BF16)
    wlr = jnp.pad(l1_w_in[:, st:st + 2 * GLA_RANK], ((0, 0), (0, LANES - 2 * GLA_RANK))).astype(BF16)
    p1 = matmul(h3, w1, BF16, 1088, 1024)
    lr = matmul(h3, wlr, F32, 1088, LANES)
    w2p = jnp.zeros((2, LANES, GLA_KW), F32)
    w2p = w2p.at[0, 0:GLA_RANK].set(l1_gate_w2[0]).at[1, GLA_RANK:2 * GLA_RANK].set(l1_gate_w2[1])
    gbp = l1_gate_b.reshape(2, 1, GLA_KW)
    gs_zero = jnp.zeros((B, GLA_H, GLA_DV, GLA_DK), F32)
    outs = []
    for d in range(2):
        _, s_c = gla_scan(p1, lr, w2p, gbp, gs_zero, d, NL, C, C)
        o_d, _ = gla_scan(p1, lr, w2p, gbp, s_c, d, 0, T, GLA_TB)
        outs.append(o_d)
    router_p = jnp.pad(l1_router, ((0, 0), (0, LANES - NE)))
    x3, h4u, route = l1_out(outs[0], outs[1], p1, x2, l1_w_out.astype(BF16), l1_o_norm.reshape(1, GLA_DV),
                            l1gt1, l1_norm2.reshape(1, D), l1sh2, l1sc2, router_p)

    pos, item_e, item_s, item_n, n_items = _route_tables(route)
    xs = moe_dispatch(pos, h4u, jnp.zeros((MOE_ROWS, D), F32))
    ys = moe_ffn(item_e, item_s, item_n, n_items, xs, l1_exp_wg, l1_exp_wu, l1_exp_wd)
    out = moe_combine(pos, x3, route, l1gt2, ys)
    return out.reshape(B, T, D)
```
